```python
import jax, jax.numpy as jnp
from jax import lax
import numpy as np

D_MODEL = 2048
BATCH = 2
SEQ = 4096
DEPTH = 4

N_EVEN = (DEPTH + 1) // 2
N_ODD = DEPTH // 2
ALPHA = (2 * DEPTH) ** 0.25
BETA = (8 * DEPTH) ** -0.25
LN_EPS = 1e-5

GLA_HEADS = 4
GLA_DK = D_MODEL // 4
GLA_DV = D_MODEL // 2
GLA_HK = GLA_DK // GLA_HEADS
GLA_HV = GLA_DV // GLA_HEADS
GLA_RANK = 16
GLA_TAU = 16.0
GLA_CHUNK = 64
POOL_WIDTH = D_MODEL // 2
POOL_WINDOWS = (2, 4, 8, 16)
POOL_GROUPS = 4
POOL_GC = POOL_WIDTH // POOL_GROUPS
SPLIT_Q = GLA_DK
SPLIT_K = 2 * GLA_DK
SPLIT_V = 2 * GLA_DK + GLA_DV
SPLIT_R = 2 * GLA_DK + 2 * GLA_DV
SPLIT_G = 2 * GLA_DK + 2 * GLA_DV + GLA_RANK
EVEN_IN = SPLIT_G + POOL_WIDTH
EVEN_MIX = GLA_DV + POOL_WIDTH
LRU_WIDTH = 5 * D_MODEL // 4
LRU_HEADS = 10
LRU_HD = LRU_WIDTH // LRU_HEADS
CONV_W = 4
LRU_C = 8.0
N_EXPERTS = 16
N_GROUPS = 4
EXP_PER_GROUP = N_EXPERTS // N_GROUPS
TOP_K = 2
D_EXPERT = D_MODEL // 2

kernel_name = "hybrid_gla_pool_rglru_groupmoe_deepnorm"


def layer_norm(x, g, b):
    xf = x.astype(jnp.float32)
    mu = jnp.mean(xf, -1, keepdims=True)
    var = jnp.mean(jnp.square(xf - mu), -1, keepdims=True)
    return ((xf - mu) * lax.rsqrt(var + LN_EPS) * g + b).astype(x.dtype)


def gla_chunked(q, k, v, log_a):
    B, S, H, dk = q.shape
    dv = v.shape[-1]
    n = S // GLA_CHUNK

    def chunks(t):
        return t.astype(jnp.float32).reshape(B, n, GLA_CHUNK, H, t.shape[-1]).transpose(0, 3, 1, 2, 4)

    q, k, v, log_a = chunks(q) * dk ** -0.5, chunks(k), chunks(v), chunks(log_a)
    b = jnp.cumsum(log_a, axis=3)
    b_last = b[:, :, :, -1:, :]
    q_in = q * jnp.exp(b)
    k_in = k * jnp.exp(-b)
    k_out = k * jnp.exp(b_last - b)
    causal = jnp.tril(jnp.ones((GLA_CHUNK, GLA_CHUNK), bool))
    scores = jnp.where(causal, jnp.einsum('bhnid,bhnjd->bhnij', q_in, k_in), 0.0)
    o_intra = jnp.einsum('bhnij,bhnjv->bhniv', scores, v)
    kv = jnp.einsum('bhnjd,bhnjv->bhndv', k_out, v)
    decay = jnp.exp(b_last[:, :, :, 0, :])

    def step(state, inp):
        dec, kv_n = inp
        return dec[..., None] * state + kv_n, state

    init = jnp.zeros((B, H, dk, dv), jnp.float32)
    _, states = lax.scan(step, init, (jnp.moveaxis(decay, 2, 0), jnp.moveaxis(kv, 2, 0)))
    states = jnp.moveaxis(states, 0, 2)
    o = o_intra + jnp.einsum('bhnid,bhndv->bhniv', q_in, states)
    return o.transpose(0, 2, 3, 1, 4).reshape(B, S, H, dv)


def multiscale_pool(u):
    B, S, _ = u.shape
    ug = u.astype(jnp.float32).reshape(B, S, POOL_GROUPS, POOL_GC)
    cs = jnp.cumsum(ug, axis=1)
    cs = jnp.concatenate([jnp.zeros_like(cs[:, :1]), cs], axis=1)
    pos = jnp.arange(S)
    outs = []
    for g, w in enumerate(POOL_WINDOWS):
        lo = jnp.maximum(pos + 1 - w, 0)
        win_sum = cs[:, 1:, g] - cs[:, lo, g]
        cnt = (pos + 1 - lo).astype(jnp.float32)
        outs.append(win_sum / cnt[None, :, None] - ug[:, :, g])
    return jnp.stack(outs, axis=2)


def gla_pool_mixer(h, w_in, w_glr, b_glr, norm_g, w_pool, pool_scale, w_out):
    B, S, _ = h.shape
    q, k, v, r, glr, u = jnp.split(h @ w_in, [SPLIT_Q, SPLIT_K, SPLIT_V, SPLIT_R, SPLIT_G], axis=-1)
    log_a = jax.nn.log_sigmoid((glr @ w_glr + b_glr).astype(jnp.float32)) / GLA_TAU
    hd = lambda t, d: t.reshape(B, S, GLA_HEADS, d)
    o = gla_chunked(hd(q, GLA_HK), hd(k, GLA_HK), hd(v, GLA_HV), hd(log_a, GLA_HK))
    o = o * lax.rsqrt(jnp.mean(o * o, -1, keepdims=True) + LN_EPS) * norm_g
    o = o.reshape(B, S, GLA_DV).astype(h.dtype) * jax.nn.silu(r)
    p = multiscale_pool(u).astype(h.dtype)
    p = jnp.einsum('bsgc,gcd->bsgd', p, w_pool).reshape(B, S, POOL_WIDTH) * pool_scale
    return jnp.concatenate([o, p], axis=-1) @ w_out


def rglru_mixer(h, w_in, conv_w, conv_b, w_rg, b_rg, w_ig, b_ig, lam, w_out):
    B, S, _ = h.shape
    gate, xr = jnp.split(h @ w_in, 2, axis=-1)
    xr = lax.conv_general_dilated(xr, conv_w[:, None, :], window_strides=(1,),
                                  padding=[(CONV_W - 1, 0)],
                                  dimension_numbers=('NWC', 'WIO', 'NWC'),
                                  feature_group_count=LRU_WIDTH) + conv_b
    xb = xr.reshape(B, S, LRU_HEADS, LRU_HD)
    r = jax.nn.sigmoid(jnp.einsum('bshi,hij->bshj', xb, w_rg).reshape(B, S, LRU_WIDTH) + b_rg)
    i = jax.nn.sigmoid(jnp.einsum('bshi,hij->bshj', xb, w_ig).reshape(B, S, LRU_WIDTH) + b_ig)
    log_a = -LRU_C * r.astype(jnp.float32) * jax.nn.softplus(-lam.astype(jnp.float32))
    a = jnp.exp(log_a)
    mult = jnp.sqrt(-jnp.expm1(2.0 * log_a)).at[:, 0].set(1.0)
    u = (xr * i).astype(jnp.float32) * mult

    def combine(left, right):
        a1, b1 = left
        a2, b2 = right
        return a1 * a2, a2 * b1 + b2

    _, hs = lax.associative_scan(combine, (a, u), axis=1)
    return (jax.nn.gelu(gate) * hs.astype(h.dtype)) @ w_out


def grouped_moe(h, w_router, router_bias, w_gate, w_up, w_down):
    B, S, D = h.shape
    t = h.reshape(B * S, D)
    scores = jax.nn.sigmoid((t @ w_router).astype(jnp.float32))
    sel = (scores + router_bias.astype(jnp.float32)).reshape(-1, N_GROUPS, EXP_PER_GROUP)
    group_score = lax.top_k(sel, TOP_K)[0].sum(-1)
    group_mask = jax.nn.one_hot(jnp.argmax(group_score, -1), N_GROUPS, dtype=jnp.bool_)[:, :, None]
    masked = jnp.where(group_mask, sel, -jnp.inf).reshape(-1, N_EXPERTS)
    _, e_idx = lax.top_k(masked, TOP_K)
    top_s = jnp.take_along_axis(scores, e_idx, axis=-1)
    wts = top_s / jnp.sum(top_s, -1, keepdims=True)
    comb = jnp.sum(jax.nn.one_hot(e_idx, N_EXPERTS, dtype=jnp.float32) * wts[..., None], axis=1).astype(t.dtype)
    out = jnp.zeros_like(t)
    for e in range(N_EXPERTS):
        y = (jax.nn.silu(t @ w_gate[e]) * (t @ w_up[e])) @ w_down[e]
        out = out + comb[:, e:e + 1] * y
    return out.reshape(B, S, D)


def setup_inputs(seed: int = 0) -> dict:
    key = jax.random.key(seed)
    ks = jax.random.split(key, 32)
    nrm = lambda k, shape, s: jax.random.normal(k, shape, jnp.float32) * s
    D = D_MODEL
    u = jax.random.uniform(ks[22], (N_ODD, LRU_WIDTH), jnp.float32, 0.9, 0.999)
    s = u ** (1.0 / LRU_C)
    return {
        "x": nrm(ks[0], (BATCH, SEQ, D), 1.0),
        "c": nrm(ks[1], (BATCH, D), 1.0),
        "w_ada": nrm(ks[2], (DEPTH, D, 6 * D), 0.1 * D ** -0.5),
        "b_ada": nrm(ks[3], (DEPTH, 6 * D), 0.02),
        "ln_mix_g": 1.0 + nrm(ks[4], (DEPTH, D), 0.02),
        "ln_mix_b": nrm(ks[5], (DEPTH, D), 0.02),
        "ln_ffn_g": 1.0 + nrm(ks[6], (DEPTH, D), 0.02),
        "ln_ffn_b": nrm(ks[7], (DEPTH, D), 0.02),
        "even_w_in": nrm(ks[8], (N_EVEN, D, EVEN_IN), D ** -0.5),
        "gla_w_glr": nrm(ks[9], (N_EVEN, GLA_RANK, GLA_DK), GLA_RANK ** -0.5),
        "gla_b_glr": nrm(ks[10], (N_EVEN, GLA_DK), 0.1),
        "gla_norm_g": 1.0 + nrm(ks[11], (N_EVEN, GLA_HEADS, GLA_HV), 0.02),
        "pool_w": nrm(ks[12], (N_EVEN, POOL_GROUPS, POOL_GC, POOL_GC), POOL_GC ** -0.5),
        "pool_scale": 1.0 + nrm(ks[13], (N_EVEN, POOL_WIDTH), 0.02),
        "even_w_out": nrm(ks[14], (N_EVEN, EVEN_MIX, D), BETA * EVEN_MIX ** -0.5),
        "odd_w_in": nrm(ks[15], (N_ODD, D, 2 * LRU_WIDTH), D ** -0.5),
        "lru_conv_w": nrm(ks[16], (N_ODD, CONV_W, LRU_WIDTH), CONV_W ** -0.5),
        "lru_conv_b": nrm(ks[17], (N_ODD, LRU_WIDTH), 0.02),
        "lru_w_rg": nrm(ks[18], (N_ODD, LRU_HEADS, LRU_HD, LRU_HD), LRU_HD ** -0.5),
        "lru_b_rg": nrm(ks[19], (N_ODD, LRU_WIDTH), 0.02),
        "lru_w_ig": nrm(ks[20], (N_ODD, LRU_HEADS, LRU_HD, LRU_HD), LRU_HD ** -0.5),
        "lru_b_ig": nrm(ks[21], (N_ODD, LRU_WIDTH), 0.02),
        "lru_lambda": jnp.log(s) - jnp.log1p(-s),
        "odd_w_out": nrm(ks[23], (N_ODD, LRU_WIDTH, D), BETA * LRU_WIDTH ** -0.5),
        "w_router": nrm(ks[24], (D, N_EXPERTS), D ** -0.5),
        "router_bias": nrm(ks[25], (N_EXPERTS,), 0.01),
        "exp_w_gate": nrm(ks[26], (DEPTH, N_EXPERTS, D, D_EXPERT), D ** -0.5),
        "exp_w_up": nrm(ks[27], (DEPTH, N_EXPERTS, D, D_EXPERT), D ** -0.5),
        "exp_w_down": nrm(ks[28], (DEPTH, N_EXPERTS, D_EXPERT, D), BETA * D_EXPERT ** -0.5),
    }


def reference(x, c, w_ada, b_ada, ln_mix_g, ln_mix_b, ln_ffn_g, ln_ffn_b,
              even_w_in, gla_w_glr, gla_b_glr, gla_norm_g, pool_w, pool_scale, even_w_out,
              odd_w_in, lru_conv_w, lru_conv_b, lru_w_rg, lru_b_rg, lru_w_ig, lru_b_ig,
              lru_lambda, odd_w_out, w_router, router_bias, exp_w_gate, exp_w_up, exp_w_down):
    mod = jnp.einsum('bd,lde->lbe', jax.nn.silu(c), w_ada) + b_ada[:, None, :]
    for l in range(DEPTH):
        sh_m, sc_m, g_m, sh_f, sc_f, g_f = jnp.split(mod[l][:, None, :], 6, axis=-1)
        h = x * (1.0 + sc_m) + sh_m
        j = l // 2
        if l % 2 == 0:
            y = gla_pool_mixer(h, even_w_in[j], gla_w_glr[j], gla_b_glr[j], gla_norm_g[j],
                               pool_w[j], pool_scale[j], even_w_out[j])
        else:
            y = rglru_mixer(h, odd_w_in[j], lru_conv_w[j], lru_conv_b[j], lru_w_rg[j], lru_b_rg[j],
                            lru_w_ig[j], lru_b_ig[j], lru_lambda[j], odd_w_out[j])
        x = layer_norm(ALPHA * x + (1.0 + g_m) * y, ln_mix_g[l], ln_mix_b[l])
        h = x * (1.0 + sc_f) + sh_f
        y = grouped_moe(h, w_router, router_bias, exp_w_gate[l], exp_w_up[l], exp_w_down[l])
        x = layer_norm(ALPHA * x + (1.0 + g_f) * y, ln_ffn_g[l], ln_ffn_b[l])
    return x
```

```python
import functools

import jax
import jax.numpy as jnp
from jax import lax
from jax.experimental import pallas as pl
from jax.experimental.pallas import tpu as pltpu

F32 = jnp.float32
BF16 = jnp.bfloat16

D_MODEL = 2048
BATCH = 2
SEQ = 4096
TOKENS = BATCH * SEQ
DEPTH = 4
ALPHA = (2 * DEPTH) ** 0.25
LN_EPS = 1e-5

GLA_HEADS = 4
GLA_DK = D_MODEL // 4
GLA_DV = D_MODEL // 2
GLA_HK = GLA_DK // GLA_HEADS
GLA_HV = GLA_DV // GLA_HEADS
GLA_RANK = 16
GLA_TAU = 16.0
GLA_CHUNK = 64
POOL_WIDTH = D_MODEL // 2
POOL_WINDOWS = (2, 4, 8, 16)
POOL_GC = POOL_WIDTH // len(POOL_WINDOWS)
POOL_HALO = 16
SPLIT_R = 2 * GLA_DK + 2 * GLA_DV
SPLIT_G = SPLIT_R + GLA_RANK
EVEN_PROJ = SPLIT_R + POOL_WIDTH

LRU_WIDTH = 5 * D_MODEL // 4
LRU_HEADS = 10
LRU_HD = LRU_WIDTH // LRU_HEADS
CONV_W = 4
CONV_HALO = 8
LRU_C = 8.0

N_EXPERTS = 16
N_GROUPS = 4
EXP_PER_GROUP = N_EXPERTS // N_GROUPS
D_EXPERT = D_MODEL // 2

LANES = 128
VMEM_LIMIT = 56 * 1024 * 1024

TM_PROJ = 1024
TN_PROJ = 512
TM_MIX = 256
TM_ROUTE = 512
TM_FFN = 256
TM_COMB = 256
N_FFN_TILES = (2 * TOKENS) // TM_FFN + N_EXPERTS
ROWS_SORTED = N_FFN_TILES * TM_FFN

_NT = (((1,), (1,)), ((), ()))


def _sigmoid(x):
    return 1.0 / (1.0 + jnp.exp(-x))


def _softplus(x):
    return jnp.maximum(x, 0.0) + jnp.log1p(jnp.exp(-jnp.abs(x)))


def _gelu_tanh(x):
    return 0.5 * x * (1.0 + jnp.tanh(0.7978845608028654 * (x + 0.044715 * (x * x * x))))


def _dot(a, b):
    return jnp.dot(a, b, preferred_element_type=F32)


def _ln_residual(x, y, gate, ln_g, ln_b):
    r = ALPHA * x + (1.0 + gate) * y
    mu = jnp.mean(r, axis=-1, keepdims=True)
    d = r - mu
    var = jnp.mean(d * d, axis=-1, keepdims=True)
    return d * lax.rsqrt(var + LN_EPS) * ln_g + ln_b


def _params(*sem):
    return pltpu.CompilerParams(dimension_semantics=sem, vmem_limit_bytes=VMEM_LIMIT)


def _mod_kernel(c_ref, w_ref, b_ref, o_ref):
    c = c_ref[...]
    sc = (c * _sigmoid(c)).astype(BF16)
    o_ref[0] = _dot(sc, w_ref[0].astype(BF16)) + b_ref[0]


def _adaln_mod(c, w_ada, b_ada):
    tn = 1024
    n = 6 * D_MODEL
    c8 = jnp.pad(c, ((0, 8 - BATCH), (0, 0)))
    out = pl.pallas_call(
        _mod_kernel,
        grid=(DEPTH, n // tn),
        in_specs=[
            pl.BlockSpec((8, D_MODEL), lambda l, j: (0, 0)),
            pl.BlockSpec((1, D_MODEL, tn), lambda l, j: (l, 0, j)),
            pl.BlockSpec((1, 1, tn), lambda l, j: (l, 0, j)),
        ],
        out_specs=pl.BlockSpec((1, 8, tn), lambda l, j: (l, 0, j)),
        out_shape=jax.ShapeDtypeStruct((DEPTH, 8, n), F32),
        compiler_params=_params("arbitrary", "arbitrary"),
        name="adaln_mod",
    )(c8, w_ada, b_ada.reshape(DEPTH, 1, n))
    return out[:, :BATCH, :].reshape(DEPTH, BATCH, 6, D_MODEL)


def _inproj_kernel(x_ref, mod_ref, w_ref, o_ref, h_scr):
    @pl.when(pl.program_id(2) == 0)
    def _():
        m = mod_ref[0]
        h_scr[...] = (x_ref[...] * (1.0 + m[1:2, :]) + m[0:1, :]).astype(BF16)

    o_ref[...] = _dot(h_scr[...], w_ref[...])


def _inproj(x2, modl, w_bf16):
    n = w_bf16.shape[1]
    ntm = SEQ // TM_PROJ
    return pl.pallas_call(
        _inproj_kernel,
        grid=(BATCH, ntm, n // TN_PROJ),
        in_specs=[
            pl.BlockSpec((TM_PROJ, D_MODEL), lambda b, i, j: (b * ntm + i, 0)),
            pl.BlockSpec((1, 6, D_MODEL), lambda b, i, j: (b, 0, 0)),
            pl.BlockSpec((D_MODEL, TN_PROJ), lambda b, i, j: (0, j)),
        ],
        out_specs=pl.BlockSpec((TM_PROJ, TN_PROJ), lambda b, i, j: (b * ntm + i, j)),
        out_shape=jax.ShapeDtypeStruct((TOKENS, n), F32),
        scratch_shapes=[pltpu.VMEM((TM_PROJ, D_MODEL), BF16)],
        compiler_params=_params("arbitrary", "arbitrary", "arbitrary"),
        name="inproj",
    )(x2, modl, w_bf16)


def _even_mixer_kernel(x_ref, proj_ref, mod_ref, wgin_ref, wglr_ref, bglr_ref, normg_ref,
                       poolw_ref, pscale_ref, wout_ref, lng_ref, lnb_ref, o_ref,
                       st_scr, ubuf_scr, cat_scr):
    tm = TM_MIX
    i = pl.program_id(1)

    @pl.when(i == 0)
    def _():
        st_scr[...] = jnp.zeros_like(st_scr)
        ubuf_scr[0:POOL_HALO, :] = jnp.zeros((POOL_HALO, POOL_WIDTH), F32)

    x = x_ref[...]
    m = mod_ref[0]
    h = (x * (1.0 + m[1:2, :]) + m[0:1, :]).astype(BF16)

    glr = _dot(h, wgin_ref[...])
    z = _dot(glr.astype(BF16), wglr_ref[...]) + bglr_ref[...]
    log_a = -_softplus(-z) * (1.0 / GLA_TAU)

    la_hi = log_a.astype(BF16)
    res = log_a - la_hi.astype(F32)
    la_mid = res.astype(BF16)
    la_lo = (res - la_mid.astype(F32)).astype(BF16)
    ri = lax.broadcasted_iota(jnp.int32, (tm, tm), 0)
    ci = lax.broadcasted_iota(jnp.int32, (tm, tm), 1)
    back = ri - ci
    tril = jnp.where(back >= 0, jnp.where(back <= (ri & (GLA_CHUNK - 1)), 1.0, 0.0), 0.0).astype(BF16)
    bcum = _dot(tril, la_hi) + _dot(tril, la_mid) + _dot(tril, la_lo)

    r64 = lax.broadcasted_iota(jnp.int32, (GLA_CHUNK, GLA_CHUNK), 0)
    c64 = lax.broadcasted_iota(jnp.int32, (GLA_CHUNK, GLA_CHUNK), 1)
    causal = c64 <= r64
    tn_dims = (((0,), (0,)), ((), ()))

    for c in range(tm // GLA_CHUNK):
        r0 = c * GLA_CHUNK
        for hh in range(GLA_HEADS):
            k0 = hh * GLA_HK
            v0 = hh * GLA_HV
            b = bcum[r0:r0 + GLA_CHUNK, k0:k0 + GLA_HK]
            b_last = b[GLA_CHUNK - 1:GLA_CHUNK, :]
            q = proj_ref[r0:r0 + GLA_CHUNK, k0:k0 + GLA_HK]
            k = proj_ref[r0:r0 + GLA_CHUNK, GLA_DK + k0:GLA_DK + k0 + GLA_HK]
            v = proj_ref[r0:r0 + GLA_CHUNK, 2 * GLA_DK + v0:2 * GLA_DK + v0 + GLA_HV].astype(BF16)
            q_in = ((q * (GLA_HK ** -0.5)) * jnp.exp(b)).astype(BF16)
            k_in = (k * jnp.exp(-b)).astype(BF16)
            k_out = (k * jnp.exp(b_last - b)).astype(BF16)
            scores = lax.dot_general(q_in, k_in, _NT, preferred_element_type=F32)
            scores = jnp.where(causal, scores, 0.0).astype(BF16)
            st = st_scr[hh]
            o = _dot(scores, v) + lax.dot_general(q_in, st.astype(BF16), _NT, preferred_element_type=F32)
            kv_t = lax.dot_general(v, k_out, tn_dims, preferred_element_type=F32)
            st_scr[hh] = st * jnp.exp(b_last) + kv_t
            ms = jnp.mean(o * o, axis=-1, keepdims=True)
            on = o * lax.rsqrt(ms + LN_EPS) * normg_ref[:, v0:v0 + GLA_HV]
            r = proj_ref[r0:r0 + GLA_CHUNK, SPLIT_R - GLA_DV + v0:SPLIT_R - GLA_DV + v0 + GLA_HV]
            cat_scr[r0:r0 + GLA_CHUNK, v0:v0 + GLA_HV] = (on * (r * _sigmoid(r))).astype(BF16)

    ubuf_scr[POOL_HALO:POOL_HALO + tm, :] = proj_ref[:, SPLIT_R:SPLIT_R + POOL_WIDTH]
    pos = lax.broadcasted_iota(jnp.int32, (tm, 1), 0) + i * tm
    for g, w in enumerate(POOL_WINDOWS):
        c0 = g * POOL_GC
        cur = ubuf_scr[POOL_HALO:POOL_HALO + tm, c0:c0 + POOL_GC]
        acc = cur
        for kk in range(1, w):
            acc = acc + ubuf_scr[POOL_HALO - kk:POOL_HALO - kk + tm, c0:c0 + POOL_GC]
        cnt = jnp.minimum(pos + 1, w).astype(F32)
        p = acc / cnt - cur
        pp = _dot(p.astype(BF16), poolw_ref[g]) * pscale_ref[:, c0:c0 + POOL_GC]
        cat_scr[:, GLA_DV + c0:GLA_DV + c0 + POOL_GC] = pp.astype(BF16)
    ubuf_scr[0:POOL_HALO, :] = ubuf_scr[tm:tm + POOL_HALO, :]

    y = _dot(cat_scr[...], wout_ref[...])
    o_ref[...] = _ln_residual(x, y, m[2:3, :], lng_ref[...], lnb_ref[...])


def _even_mixer(x2, proj, modl, wgin, wglr, bglr, normg, poolw, pscale, wout, lng, lnb):
    nt = SEQ // TM_MIX
    tok = lambda b, i: (b * nt + i, 0)
    const2 = lambda b, i: (0, 0)
    const3 = lambda b, i: (0, 0, 0)
    return pl.pallas_call(
        _even_mixer_kernel,
        grid=(BATCH, nt),
        in_specs=[
            pl.BlockSpec((TM_MIX, D_MODEL), tok),
            pl.BlockSpec((TM_MIX, EVEN_PROJ), tok),
            pl.BlockSpec((1, 6, D_MODEL), lambda b, i: (b, 0, 0)),
            pl.BlockSpec((D_MODEL, LANES), const2),
            pl.BlockSpec((LANES, GLA_DK), const2),
            pl.BlockSpec((1, GLA_DK), const2),
            pl.BlockSpec((1, GLA_DV), const2),
            pl.BlockSpec((len(POOL_WINDOWS), POOL_GC, POOL_GC), const3),
            pl.BlockSpec((1, POOL_WIDTH), const2),
            pl.BlockSpec((D_MODEL, D_MODEL), const2),
            pl.BlockSpec((1, D_MODEL), const2),
            pl.BlockSpec((1, D_MODEL), const2),
        ],
        out_specs=pl.BlockSpec((TM_MIX, D_MODEL), tok),
        out_shape=jax.ShapeDtypeStruct((TOKENS, D_MODEL), F32),
        scratch_shapes=[
            pltpu.VMEM((GLA_HEADS, GLA_HV, GLA_HK), F32),
            pltpu.VMEM((POOL_HALO + TM_MIX, POOL_WIDTH), F32),
            pltpu.VMEM((TM_MIX, D_MODEL), BF16),
        ],
        compiler_params=_params("arbitrary", "arbitrary"),
        name="even_mixer",
    )(x2, proj, modl, wgin, wglr, bglr, normg, poolw, pscale, wout, lng, lnb)


def _shift_rows(x, d, fill):
    n = x.shape[0]
    if d % 8 == 0:
        return jnp.concatenate([jnp.full((d, x.shape[1]), fill, x.dtype), x[:n - d]], axis=0)
    rolled = pltpu.roll(x, d, 0)
    ridx = lax.broadcasted_iota(jnp.int32, x.shape, 0)
    return jnp.where(ridx >= d, rolled, fill)


def _odd_mixer_kernel(x_ref, proj_ref, mod_ref, convw_ref, convb_ref, wrg_ref, brg_ref,
                      wig_ref, big_ref, lam_ref, wout_ref, lng_ref, lnb_ref, o_ref,
                      xbuf_scr, hc_scr, z_scr):
    tm = TM_MIX
    i = pl.program_id(1)

    @pl.when(i == 0)
    def _():
        xbuf_scr[0:CONV_HALO, :] = jnp.zeros((CONV_HALO, LRU_WIDTH), F32)
        hc_scr[...] = jnp.zeros_like(hc_scr)

    xbuf_scr[CONV_HALO:CONV_HALO + tm, :] = proj_ref[:, LRU_WIDTH:2 * LRU_WIDTH]
    pos = lax.broadcasted_iota(jnp.int32, (tm, 1), 0) + i * tm

    for hb in range(LRU_HEADS):
        c0 = hb * LRU_HD
        xc = convb_ref[:, c0:c0 + LRU_HD]
        for kk in range(CONV_W):
            s0 = CONV_HALO - (CONV_W - 1) + kk
            xc = xc + convw_ref[kk:kk + 1, c0:c0 + LRU_HD] * xbuf_scr[s0:s0 + tm, c0:c0 + LRU_HD]
        xcb = xc.astype(BF16)
        rg = _sigmoid(_dot(xcb, wrg_ref[hb]) + brg_ref[:, c0:c0 + LRU_HD])
        ig = _sigmoid(_dot(xcb, wig_ref[hb]) + big_ref[:, c0:c0 + LRU_HD])
        log_a = (-LRU_C) * rg * _softplus(-lam_ref[:, c0:c0 + LRU_HD])
        a = jnp.exp(log_a)
        th = jnp.tanh(log_a)
        mult = jnp.sqrt(-2.0 * th / (1.0 - th))
        mult = jnp.where(pos == 0, 1.0, mult)
        u = (xc * ig) * mult

        d = 1
        while d < tm:
            a_s = _shift_rows(a, d, 1.0)
            u_s = _shift_rows(u, d, 0.0)
            u = a * u_s + u
            a = a * a_s
            d *= 2
        hseq = a * hc_scr[:, c0:c0 + LRU_HD] + u
        hc_scr[:, c0:c0 + LRU_HD] = hseq[tm - 1:tm, :]
        gate = proj_ref[:, c0:c0 + LRU_HD]
        z_scr[:, c0:c0 + LRU_HD] = (_gelu_tanh(gate) * hseq).astype(BF16)

    xbuf_scr[0:CONV_HALO, :] = xbuf_scr[tm:tm + CONV_HALO, :]
    y = _dot(z_scr[...], wout_ref[...])
    m = mod_ref[0]
    o_ref[...] = _ln_residual(x_ref[...], y, m[2:3, :], lng_ref[...], lnb_ref[...])


def _odd_mixer(x2, proj, modl, convw, convb, wrg, brg, wig, big, lam, wout, lng, lnb):
    nt = SEQ // TM_MIX
    tok = lambda b, i: (b * nt + i, 0)
    const2 = lambda b, i: (0, 0)
    const3 = lambda b, i: (0, 0, 0)
    return pl.pallas_call(
        _odd_mixer_kernel,
        grid=(BATCH, nt),
        in_specs=[
            pl.BlockSpec((TM_MIX, D_MODEL), tok),
            pl.BlockSpec((TM_MIX, 2 * LRU_WIDTH), tok),
            pl.BlockSpec((1, 6, D_MODEL), lambda b, i: (b, 0, 0)),
            pl.BlockSpec((CONV_W, LRU_WIDTH), const2),
            pl.BlockSpec((1, LRU_WIDTH), const2),
            pl.BlockSpec((LRU_HEADS, LRU_HD, LRU_HD), const3),
            pl.BlockSpec((1, LRU_WIDTH), const2),
            pl.BlockSpec((LRU_HEADS, LRU_HD, LRU_HD), const3),
            pl.BlockSpec((1, LRU_WIDTH), const2),
            pl.BlockSpec((1, LRU_WIDTH), const2),
            pl.BlockSpec((LRU_WIDTH, D_MODEL), const2),
            pl.BlockSpec((1, D_MODEL), const2),
            pl.BlockSpec((1, D_MODEL), const2),
        ],
        out_specs=pl.BlockSpec((TM_MIX, D_MODEL), tok),
        out_shape=jax.ShapeDtypeStruct((TOKENS, D_MODEL), F32),
        scratch_shapes=[
            pltpu.VMEM((CONV_HALO + TM_MIX, LRU_WIDTH), F32),
            pltpu.VMEM((1, LRU_WIDTH), F32),
            pltpu.VMEM((TM_MIX, LRU_WIDTH), BF16),
        ],
        compiler_params=_params("arbitrary", "arbitrary"),
        name="odd_mixer",
    )(x2, proj, modl, convw, convb, wrg, brg, wig, big, lam, wout, lng, lnb)


def _first_max4(v):
    m = jnp.maximum(jnp.maximum(v[0], v[1]), jnp.maximum(v[2], v[3]))
    idx = jnp.where(v[0] == m, 0, jnp.where(v[1] == m, 1, jnp.where(v[2] == m, 2, 3)))
    return m, idx


def _router_kernel(x_ref, mod_ref, wrt_ref, bias_ref, hf_ref, eidx_ref, wts_ref):
    m = mod_ref[0]
    h = x_ref[...] * (1.0 + m[4:5, :]) + m[3:4, :]
    hf_ref[...] = h

    h_hi = h.astype(BF16)
    h_lo = (h - h_hi.astype(F32)).astype(BF16)
    w = wrt_ref[...]
    w_hi = w.astype(BF16)
    w_lo = (w - w_hi.astype(F32)).astype(BF16)
    dg = functools.partial(lax.dot_general, dimension_numbers=_NT, preferred_element_type=F32)
    logits = dg(w_hi, h_hi) + dg(w_hi, h_lo) + dg(w_lo, h_hi)
    score = _sigmoid(logits)
    sel = score + bias_ref[...]

    neg = -jnp.inf
    s_rows = [sel[e:e + 1, :] for e in range(N_EXPERTS)]
    group_score, first, second = [], [], []
    for g in range(N_GROUPS):
        v = s_rows[g * EXP_PER_GROUP:(g + 1) * EXP_PER_GROUP]
        m1, i1 = _first_max4(v)
        v2 = [jnp.where(i1 == j, neg, v[j]) for j in range(EXP_PER_GROUP)]
        m2, i2 = _first_max4(v2)
        group_score.append(m1 + m2)
        first.append(i1)
        second.append(i2)
    _, gbest = _first_max4(group_score)
    pick = lambda rows: jnp.where(gbest == 0, rows[0], jnp.where(gbest == 1, rows[1],
                                  jnp.where(gbest == 2, rows[2], rows[3])))
    e1 = gbest * EXP_PER_GROUP + pick(first)
    e2 = gbest * EXP_PER_GROUP + pick(second)
    s1 = jnp.zeros_like(s_rows[0])
    s2 = jnp.zeros_like(s_rows[0])
    for e in range(N_EXPERTS):
        row = score[e:e + 1, :]
        s1 = s1 + jnp.where(e1 == e, row, 0.0)
        s2 = s2 + jnp.where(e2 == e, row, 0.0)
    den = s1 + s2
    eidx_ref[0:1, :] = e1
    eidx_ref[1:2, :] = e2
    wts_ref[0:1, :] = s1 / den
    wts_ref[1:2, :] = s2 / den


def _router(x2, modl, wrt, bias_col):
    nt = SEQ // TM_ROUTE
    tok = lambda b, i: (b * nt + i, 0)
    tokT = lambda b, i: (0, b * nt + i)
    return pl.pallas_call(
        _router_kernel,
        grid=(BATCH, nt),
        in_specs=[
            pl.BlockSpec((TM_ROUTE, D_MODEL), tok),
            pl.BlockSpec((1, 6, D_MODEL), lambda b, i: (b, 0, 0)),
            pl.BlockSpec((N_EXPERTS, D_MODEL), lambda b, i: (0, 0)),
            pl.BlockSpec((N_EXPERTS, 1), lambda b, i: (0, 0)),
        ],
        out_specs=[
            pl.BlockSpec((TM_ROUTE, D_MODEL), tok),
            pl.BlockSpec((2, TM_ROUTE), tokT),
            pl.BlockSpec((2, TM_ROUTE), tokT),
        ],
        out_shape=[
            jax.ShapeDtypeStruct((TOKENS, D_MODEL), F32),
            jax.ShapeDtypeStruct((2, TOKENS), jnp.int32),
            jax.ShapeDtypeStruct((2, TOKENS), F32),
        ],
        compiler_params=_params("arbitrary", "arbitrary"),
        name="moe_router",
    )(x2, modl, wrt, bias_col)


def _rank_kernel(e_ref, rank_ref, cnt_ref, run_scr):
    tq = TM_ROUTE
    i = pl.program_id(0)

    @pl.when(i == 0)
    def _():
        run_scr[...] = jnp.zeros_like(run_scr)

    e = e_ref[...]
    eid = lax.broadcasted_iota(jnp.int32, (N_EXPERTS, tq), 0)
    oh0 = jnp.where(eid == e[0:1, :], 1.0, 0.0)
    oh1 = jnp.where(eid == e[1:2, :], 1.0, 0.0)
    rj = lax.broadcasted_iota(jnp.int32, (tq, tq), 0)
    ct = lax.broadcasted_iota(jnp.int32, (tq, tq), 1)
    upper = jnp.where(rj <= ct, 1.0, 0.0).astype(BF16)
    pre0 = _dot(oh0.astype(BF16), upper)
    pre1 = _dot(oh1.astype(BF16), upper)
    tot0 = pre0[:, tq - 1:tq]
    tot1 = pre1[:, tq - 1:tq]
    run = run_scr[:, 0:1]
    r0 = jnp.sum(oh0 * (run + pre0 - 1.0), axis=0, keepdims=True)
    r1 = jnp.sum(oh1 * (run + tot0 + pre1 - 1.0), axis=0, keepdims=True)
    rank_ref[0:1, :] = r0.astype(jnp.int32)
    rank_ref[1:2, :] = r1.astype(jnp.int32)
    new_run = jnp.broadcast_to(run + tot0 + tot1, (N_EXPERTS, LANES))
    run_scr[...] = new_run
    cnt_ref[...] = new_run.astype(jnp.int32)


def _rank(eidx):
    nt = TOKENS // TM_ROUTE
    return pl.pallas_call(
        _rank_kernel,
        grid=(nt,),
        in_specs=[pl.BlockSpec((2, TM_ROUTE), lambda i: (0, i))],
        out_specs=[
            pl.BlockSpec((2, TM_ROUTE), lambda i: (0, i)),
            pl.BlockSpec((N_EXPERTS, LANES), lambda i: (0, 0)),
        ],
        out_shape=[
            jax.ShapeDtypeStruct((2, TOKENS), jnp.int32),
            jax.ShapeDtypeStruct((N_EXPERTS, LANES), jnp.int32),
        ],
        scratch_shapes=[pltpu.VMEM((N_EXPERTS, LANES), F32)],
        compiler_params=_params("arbitrary"),
        name="moe_rank",
    )(eidx)


def _row_copy(src_ref, src_row, dst_ref, dst_row, sem):
    return pltpu.make_async_copy(src_ref.at[pl.ds(src_row, 1), :], dst_ref.at[pl.ds(dst_row, 1), :], sem)


def _dispatch_kernel(pos_ref, hf_hbm, init_hbm, xs_hbm, sem):
    del init_hbm
    tq = TM_ROUTE
    base = pl.program_id(0) * tq

    def start(t, carry):
        _row_copy(hf_hbm, base + t, xs_hbm, pos_ref[t], sem).start()
        _row_copy(hf_hbm, base + t, xs_hbm, pos_ref[tq + t], sem).start()
        return carry

    lax.fori_loop(0, tq, start, 0)

    def wait(t, carry):
        _row_copy(hf_hbm, 0, xs_hbm, 0, sem).wait()
        _row_copy(hf_hbm, 0, xs_hbm, 0, sem).wait()
        return carry

    lax.fori_loop(0, tq, wait, 0)


def _dispatch(pos_tiles, hf):
    nt = TOKENS // TM_ROUTE
    zeros = jnp.zeros((ROWS_SORTED, D_MODEL), F32)
    return pl.pallas_call(
        _dispatch_kernel,
        grid=(nt,),
        in_specs=[
            pl.BlockSpec((2 * TM_ROUTE,), lambda i: (i,), memory_space=pltpu.SMEM),
            pl.BlockSpec(memory_space=pl.ANY),
            pl.BlockSpec(memory_space=pl.ANY),
        ],
        out_specs=pl.BlockSpec(memory_space=pl.ANY),
        out_shape=jax.ShapeDtypeStruct((ROWS_SORTED, D_MODEL), F32),
        scratch_shapes=[pltpu.SemaphoreType.DMA(())],
        input_output_aliases={2: 0},
        compiler_params=_params("arbitrary"),
        name="moe_dispatch",
    )(pos_tiles, hf, zeros)


def _ffn_kernel(te_ref, tv_ref, x_ref, wg_ref, wu_ref, wd_ref, y_ref):
    del te_ref
    i = pl.program_id(0)

    @pl.when(tv_ref[i] != 0)
    def _():
        xb = x_ref[...].astype(BF16)
        g = _dot(xb, wg_ref[0])
        u = _dot(xb, wu_ref[0])
        act = ((g * _sigmoid(g)) * u).astype(BF16)
        y_ref[...] = _dot(act, wd_ref[0])

    @pl.when(tv_ref[i] == 0)
    def _():
        y_ref[...] = jnp.zeros_like(y_ref)


def _ffn(tile_expert, tile_valid, xs, wg, wu, wd):
    grid_spec = pltpu.PrefetchScalarGridSpec(
        num_scalar_prefetch=2,
        grid=(N_FFN_TILES,),
        in_specs=[
            pl.BlockSpec((TM_FFN, D_MODEL), lambda i, te, tv: (i, 0)),
            pl.BlockSpec((1, D_MODEL, D_EXPERT), lambda i, te, tv: (te[i], 0, 0)),
            pl.BlockSpec((1, D_MODEL, D_EXPERT), lambda i, te, tv: (te[i], 0, 0)),
            pl.BlockSpec((1, D_EXPERT, D_MODEL), lambda i, te, tv: (te[i], 0, 0)),
        ],
        out_specs=pl.BlockSpec((TM_FFN, D_MODEL), lambda i, te, tv: (i, 0)),
    )
    return pl.pallas_call(
        _ffn_kernel,
        grid_spec=grid_spec,
        out_shape=jax.ShapeDtypeStruct((ROWS_SORTED, D_MODEL), F32),
        compiler_params=_params("arbitrary"),
        name="moe_ffn",
    )(tile_expert, tile_valid, xs, wg, wu, wd)


def _combine_kernel(pos_ref, x_ref, mod_ref, w_ref, lng_ref, lnb_ref, ys_hbm, o_ref, buf0, buf1, sem):
    tm = TM_COMB

    def start(t, carry):
        _row_copy(ys_hbm, pos_ref[t], buf0, t, sem).start()
        _row_copy(ys_hbm, pos_ref[tm + t], buf1, t, sem).start()
        return carry

    lax.fori_loop(0, tm, start, 0)

    def wait(t, carry):
        _row_copy(ys_hbm, 0, buf0, 0, sem).wait()
        _row_copy(ys_hbm, 0, buf1, 0, sem).wait()
        return carry

    lax.fori_loop(0, tm, wait, 0)

    w = w_ref[...]
    y = w[:, 0:1] * buf0[...] + w[:, 1:2] * buf1[...]
    m = mod_ref[0]
    o_ref[...] = _ln_residual(x_ref[...], y, m[5:6, :], lng_ref[...], lnb_ref[...])


def _combine(pos_tiles, x2, modl, wts_col, lng, lnb, ys):
    nt = SEQ // TM_COMB
    tok = lambda b, i: (b * nt + i, 0)
    return pl.pallas_call(
        _combine_kernel,
        grid=(BATCH, nt),
        in_specs=[
            pl.BlockSpec((2 * TM_COMB,), lambda b, i: (b * nt + i,), memory_space=pltpu.SMEM),
            pl.BlockSpec((TM_COMB, D_MODEL), tok),
            pl.BlockSpec((1, 6, D_MODEL), lambda b, i: (b, 0, 0)),
            pl.BlockSpec((TM_COMB, 2), tok),
            pl.BlockSpec((1, D_MODEL), lambda b, i: (0, 0)),
            pl.BlockSpec((1, D_MODEL), lambda b, i: (0, 0)),
            pl.BlockSpec(memory_space=pl.ANY),
        ],
        out_specs=pl.BlockSpec((TM_COMB, D_MODEL), tok),
        out_shape=jax.ShapeDtypeStruct((TOKENS, D_MODEL), F32),
        scratch_shapes=[
            pltpu.VMEM((TM_COMB, D_MODEL), F32),
            pltpu.VMEM((TM_COMB, D_MODEL), F32),
            pltpu.SemaphoreType.DMA(()),
        ],
        compiler_params=_params("arbitrary", "arbitrary"),
        name="moe_combine",
    )(pos_tiles, x2, modl, wts_col, lng, lnb, ys)


def _tile_positions(pos, tile):
    return pos.reshape(2, TOKENS // tile, tile).transpose(1, 0, 2).reshape(-1)


def _moe(x2, modl, wrt, bias_col, wg, wu, wd, lng, lnb):
    hf, eidx, wts = _router(x2, modl, wrt, bias_col)
    rank, cnt = _rank(eidx)
    counts = cnt[:, 0]
    tiles_e = (counts + TM_FFN - 1) // TM_FFN
    tile_end = jnp.cumsum(tiles_e)
    offs = (tile_end - tiles_e) * TM_FFN
    pos = jnp.take(offs, eidx) + rank
    n_used = tile_end[-1]
    tile_ids = jnp.arange(N_FFN_TILES, dtype=jnp.int32)
    tile_valid = (tile_ids < n_used).astype(jnp.int32)
    clamped = jnp.minimum(tile_ids, n_used - 1)
    tile_expert = jnp.minimum(jnp.searchsorted(tile_end, clamped, side="right"), N_EXPERTS - 1).astype(jnp.int32)
    xs = _dispatch(_tile_positions(pos, TM_ROUTE), hf)
    ys = _ffn(tile_expert, tile_valid, xs, wg, wu, wd)
    return _combine(_tile_positions(pos, TM_COMB), x2, modl, wts.T, lng, lnb, ys)


def kernel(x, c, w_ada, b_ada, ln_mix_g, ln_mix_b, ln_ffn_g, ln_ffn_b, even_w_in, gla_w_glr, gla_b_glr, gla_norm_g, pool_w, pool_scale, even_w_out, odd_w_in, lru_conv_w, lru_conv_b, lru_w_rg, lru_b_rg, lru_w_ig, lru_b_ig, lru_lambda, odd_w_out, w_router, router_bias, exp_w_gate, exp_w_up, exp_w_down):
    mod = _adaln_mod(c, w_ada, b_ada)
    x2 = x.reshape(TOKENS, D_MODEL)
    wrt = w_router.T
    bias_col = router_bias.reshape(N_EXPERTS, 1)
    row = lambda a: a.reshape(1, -1)
    for l in range(DEPTH):
        j = l // 2
        modl = mod[l]
        if l % 2 == 0:
            w_in = even_w_in[j]
            w_main = jnp.concatenate([w_in[:, :SPLIT_R], w_in[:, SPLIT_G:]], axis=1).astype(BF16)
            wgin = jnp.pad(w_in[:, SPLIT_R:SPLIT_G], ((0, 0), (0, LANES - GLA_RANK))).astype(BF16)
            wglr = jnp.pad(gla_w_glr[j], ((0, LANES - GLA_RANK), (0, 0))).astype(BF16)
            proj = _inproj(x2, modl, w_main)
            x2 = _even_mixer(x2, proj, modl, wgin, wglr, row(gla_b_glr[j]), row(gla_norm_g[j]),
                             pool_w[j].astype(BF16), row(pool_scale[j]), even_w_out[j].astype(BF16),
                             row(ln_mix_g[l]), row(ln_mix_b[l]))
        else:
            proj = _inproj(x2, modl, odd_w_in[j].astype(BF16))
            x2 = _odd_mixer(x2, proj, modl, lru_conv_w[j], row(lru_conv_b[j]),
                            lru_w_rg[j].astype(BF16), row(lru_b_rg[j]), lru_w_ig[j].astype(BF16),
                            row(lru_b_ig[j]), row(lru_lambda[j]), odd_w_out[j].astype(BF16),
                            row(ln_mix_g[l]), row(ln_mix_b[l]))
        x2 = _moe(x2, modl, wrt, bias_col, exp_w_gate[l].astype(BF16), exp_w_up[l].astype(BF16),
                  exp_w_down[l].astype(BF16), row(ln_ffn_g[l]), row(ln_ffn_b[l]))
    return x2.reshape(BATCH, SEQ, D_MODEL)
```

```python
import functools

import jax
import jax.numpy as jnp
from jax import lax
from jax.experimental import pallas as pl
from jax.experimental.pallas import tpu as pltpu

F32 = jnp.float32
BF16 = jnp.bfloat16

D_MODEL = 2048
BATCH = 2
SEQ = 4096
TOKENS = BATCH * SEQ
DEPTH = 4
ALPHA = (2 * DEPTH) ** 0.25
LN_EPS = 1e-5

GLA_HEADS = 4
GLA_DK = D_MODEL // 4
GLA_DV = D_MODEL // 2
GLA_HK = GLA_DK // GLA_HEADS
GLA_HV = GLA_DV // GLA_HEADS
GLA_RANK = 16
GLA_TAU = 16.0
GLA_CHUNK = 64
POOL_WIDTH = D_MODEL // 2
POOL_WINDOWS = (2, 4, 8, 16)
POOL_GC = POOL_WIDTH // len(POOL_WINDOWS)
POOL_HALO = 16
SPLIT_R = 2 * GLA_DK + 2 * GLA_DV
SPLIT_G = SPLIT_R + GLA_RANK
EVEN_PROJ = SPLIT_R + POOL_WIDTH

LRU_WIDTH = 5 * D_MODEL // 4
LRU_HEADS = 10
LRU_HD = LRU_WIDTH // LRU_HEADS
CONV_W = 4
CONV_HALO = 8
LRU_C = 8.0

N_EXPERTS = 16
N_GROUPS = 4
EXP_PER_GROUP = N_EXPERTS // N_GROUPS
D_EXPERT = D_MODEL // 2

LANES = 128
VMEM_LIMIT = 56 * 1024 * 1024

TM_PROJ = 1024
TN_PROJ = 512
TM_MIX = 256
TM_ROUTE = 512
TM_COMB = 256
SEG_SUB = 256
SEG_CAP = 5 * SEG_SUB
MAX_SEG_PER_EXPERT = -(-TOKENS // SEG_CAP)
N_SEG = N_EXPERTS + (2 * TOKENS) // SEG_CAP + 1
INV_CHUNK = 2048
INV_SPARE = 8

_NT = (((1,), (1,)), ((), ()))


def _sigmoid(x):
    return 1.0 / (1.0 + jnp.exp(-x))


def _softplus(x):
    return jnp.maximum(x, 0.0) + jnp.log1p(jnp.exp(-jnp.abs(x)))


def _gelu_tanh(x):
    return 0.5 * x * (1.0 + jnp.tanh(0.7978845608028654 * (x + 0.044715 * (x * x * x))))


def _dot(a, b):
    return jnp.dot(a, b, preferred_element_type=F32)


def _ln_residual(x, y, gate, ln_g, ln_b):
    r = ALPHA * x + (1.0 + gate) * y
    mu = jnp.mean(r, axis=-1, keepdims=True)
    d = r - mu
    var = jnp.mean(d * d, axis=-1, keepdims=True)
    return d * lax.rsqrt(var + LN_EPS) * ln_g + ln_b


def _params(*sem):
    return pltpu.CompilerParams(dimension_semantics=sem, vmem_limit_bytes=VMEM_LIMIT)


def _mod_kernel(c_ref, w_ref, b_ref, o_ref):
    c = c_ref[...]
    sc = (c * _sigmoid(c)).astype(BF16)
    o_ref[0] = _dot(sc, w_ref[0].astype(BF16)) + b_ref[0]


def _adaln_mod(c, w_ada, b_ada):
    tn = 1024
    n = 6 * D_MODEL
    c8 = jnp.pad(c, ((0, 8 - BATCH), (0, 0)))
    out = pl.pallas_call(
        _mod_kernel,
        grid=(DEPTH, n // tn),
        in_specs=[
            pl.BlockSpec((8, D_MODEL), lambda l, j: (0, 0)),
            pl.BlockSpec((1, D_MODEL, tn), lambda l, j: (l, 0, j)),
            pl.BlockSpec((1, 1, tn), lambda l, j: (l, 0, j)),
        ],
        out_specs=pl.BlockSpec((1, 8, tn), lambda l, j: (l, 0, j)),
        out_shape=jax.ShapeDtypeStruct((DEPTH, 8, n), F32),
        compiler_params=_params("arbitrary", "arbitrary"),
        name="adaln_mod",
    )(c8, w_ada, b_ada.reshape(DEPTH, 1, n))
    return out[:, :BATCH, :].reshape(DEPTH, BATCH, 6, D_MODEL)


def _inproj_kernel(x_ref, mod_ref, w_ref, o_ref, h_scr):
    @pl.when(pl.program_id(2) == 0)
    def _():
        m = mod_ref[0]
        h_scr[...] = (x_ref[...] * (1.0 + m[1:2, :]) + m[0:1, :]).astype(BF16)

    o_ref[...] = _dot(h_scr[...], w_ref[...])


def _inproj(x2, modl, w_bf16):
    n = w_bf16.shape[1]
    ntm = SEQ // TM_PROJ
    return pl.pallas_call(
        _inproj_kernel,
        grid=(BATCH, ntm, n // TN_PROJ),
        in_specs=[
            pl.BlockSpec((TM_PROJ, D_MODEL), lambda b, i, j: (b * ntm + i, 0)),
            pl.BlockSpec((1, 6, D_MODEL), lambda b, i, j: (b, 0, 0)),
            pl.BlockSpec((D_MODEL, TN_PROJ), lambda b, i, j: (0, j)),
        ],
        out_specs=pl.BlockSpec((TM_PROJ, TN_PROJ), lambda b, i, j: (b * ntm + i, j)),
        out_shape=jax.ShapeDtypeStruct((TOKENS, n), F32),
        scratch_shapes=[pltpu.VMEM((TM_PROJ, D_MODEL), BF16)],
        compiler_params=_params("arbitrary", "arbitrary", "arbitrary"),
        name="inproj",
    )(x2, modl, w_bf16)


def _even_mixer_kernel(x_ref, proj_ref, mod_ref, wgin_ref, wglr_ref, bglr_ref, normg_ref,
                       poolw_ref, pscale_ref, wout_ref, lng_ref, lnb_ref, o_ref,
                       st_scr, ubuf_scr, cat_scr):
    tm = TM_MIX
    i = pl.program_id(1)

    @pl.when(i == 0)
    def _():
        st_scr[...] = jnp.zeros_like(st_scr)
        ubuf_scr[0:POOL_HALO, :] = jnp.zeros((POOL_HALO, POOL_WIDTH), F32)

    x = x_ref[...]
    m = mod_ref[0]
    h = (x * (1.0 + m[1:2, :]) + m[0:1, :]).astype(BF16)

    glr = _dot(h, wgin_ref[...])
    z = _dot(glr.astype(BF16), wglr_ref[...]) + bglr_ref[...]
    log_a = -_softplus(-z) * (1.0 / GLA_TAU)

    la_hi = log_a.astype(BF16)
    res = log_a - la_hi.astype(F32)
    la_mid = res.astype(BF16)
    la_lo = (res - la_mid.astype(F32)).astype(BF16)
    ri = lax.broadcasted_iota(jnp.int32, (tm, tm), 0)
    ci = lax.broadcasted_iota(jnp.int32, (tm, tm), 1)
    back = ri - ci
    tril = jnp.where(back >= 0, jnp.where(back <= (ri & (GLA_CHUNK - 1)), 1.0, 0.0), 0.0).astype(BF16)
    bcum = _dot(tril, la_hi) + _dot(tril, la_mid) + _dot(tril, la_lo)

    r64 = lax.broadcasted_iota(jnp.int32, (GLA_CHUNK, GLA_CHUNK), 0)
    c64 = lax.broadcasted_iota(jnp.int32, (GLA_CHUNK, GLA_CHUNK), 1)
    causal = c64 <= r64
    tn_dims = (((0,), (0,)), ((), ()))

    for c in range(tm // GLA_CHUNK):
        r0 = c * GLA_CHUNK
        for hh in range(GLA_HEADS):
            k0 = hh * GLA_HK
            v0 = hh * GLA_HV
            b = bcum[r0:r0 + GLA_CHUNK, k0:k0 + GLA_HK]
            b_last = b[GLA_CHUNK - 1:GLA_CHUNK, :]
            q = proj_ref[r0:r0 + GLA_CHUNK, k0:k0 + GLA_HK]
            k = proj_ref[r0:r0 + GLA_CHUNK, GLA_DK + k0:GLA_DK + k0 + GLA_HK]
            v = proj_ref[r0:r0 + GLA_CHUNK, 2 * GLA_DK + v0:2 * GLA_DK + v0 + GLA_HV].astype(BF16)
            q_in = ((q * (GLA_HK ** -0.5)) * jnp.exp(b)).astype(BF16)
            k_in = (k * jnp.exp(-b)).astype(BF16)
            k_out = (k * jnp.exp(b_last - b)).astype(BF16)
            scores = lax.dot_general(q_in, k_in, _NT, preferred_element_type=F32)
            scores = jnp.where(causal, scores, 0.0).astype(BF16)
            st = st_scr[hh]
            o = _dot(scores, v) + lax.dot_general(q_in, st.astype(BF16), _NT, preferred_element_type=F32)
            kv_t = lax.dot_general(v, k_out, tn_dims, preferred_element_type=F32)
            st_scr[hh] = st * jnp.exp(b_last) + kv_t
            ms = jnp.mean(o * o, axis=-1, keepdims=True)
            on = o * lax.rsqrt(ms + LN_EPS) * normg_ref[:, v0:v0 + GLA_HV]
            r = proj_ref[r0:r0 + GLA_CHUNK, SPLIT_R - GLA_DV + v0:SPLIT_R - GLA_DV + v0 + GLA_HV]
            cat_scr[r0:r0 + GLA_CHUNK, v0:v0 + GLA_HV] = (on * (r * _sigmoid(r))).astype(BF16)

    ubuf_scr[POOL_HALO:POOL_HALO + tm, :] = proj_ref[:, SPLIT_R:SPLIT_R + POOL_WIDTH]
    pos = lax.broadcasted_iota(jnp.int32, (tm, 1), 0) + i * tm
    for g, w in enumerate(POOL_WINDOWS):
        c0 = g * POOL_GC
        cur = ubuf_scr[POOL_HALO:POOL_HALO + tm, c0:c0 + POOL_GC]
        acc = cur
        for kk in range(1, w):
            acc = acc + ubuf_scr[POOL_HALO - kk:POOL_HALO - kk + tm, c0:c0 + POOL_GC]
        cnt = jnp.minimum(pos + 1, w).astype(F32)
        p = acc / cnt - cur
        pp = _dot(p.astype(BF16), poolw_ref[g]) * pscale_ref[:, c0:c0 + POOL_GC]
        cat_scr[:, GLA_DV + c0:GLA_DV + c0 + POOL_GC] = pp.astype(BF16)
    ubuf_scr[0:POOL_HALO, :] = ubuf_scr[tm:tm + POOL_HALO, :]

    y = _dot(cat_scr[...], wout_ref[...])
    o_ref[...] = _ln_residual(x, y, m[2:3, :], lng_ref[...], lnb_ref[...])


def _even_mixer(x2, proj, modl, wgin, wglr, bglr, normg, poolw, pscale, wout, lng, lnb):
    nt = SEQ // TM_MIX
    tok = lambda b, i: (b * nt + i, 0)
    const2 = lambda b, i: (0, 0)
    const3 = lambda b, i: (0, 0, 0)
    return pl.pallas_call(
        _even_mixer_kernel,
        grid=(BATCH, nt),
        in_specs=[
            pl.BlockSpec((TM_MIX, D_MODEL), tok),
            pl.BlockSpec((TM_MIX, EVEN_PROJ), tok),
            pl.BlockSpec((1, 6, D_MODEL), lambda b, i: (b, 0, 0)),
            pl.BlockSpec((D_MODEL, LANES), const2),
            pl.BlockSpec((LANES, GLA_DK), const2),
            pl.BlockSpec((1, GLA_DK), const2),
            pl.BlockSpec((1, GLA_DV), const2),
            pl.BlockSpec((len(POOL_WINDOWS), POOL_GC, POOL_GC), const3),
            pl.BlockSpec((1, POOL_WIDTH), const2),
            pl.BlockSpec((D_MODEL, D_MODEL), const2),
            pl.BlockSpec((1, D_MODEL), const2),
            pl.BlockSpec((1, D_MODEL), const2),
        ],
        out_specs=pl.BlockSpec((TM_MIX, D_MODEL), tok),
        out_shape=jax.ShapeDtypeStruct((TOKENS, D_MODEL), F32),
        scratch_shapes=[
            pltpu.VMEM((GLA_HEADS, GLA_HV, GLA_HK), F32),
            pltpu.VMEM((POOL_HALO + TM_MIX, POOL_WIDTH), F32),
            pltpu.VMEM((TM_MIX, D_MODEL), BF16),
        ],
        compiler_params=_params("arbitrary", "arbitrary"),
        name="even_mixer",
    )(x2, proj, modl, wgin, wglr, bglr, normg, poolw, pscale, wout, lng, lnb)


def _odd_mixer_kernel(x_ref, proj_ref, mod_ref, convw_ref, convb_ref, wrg_ref, brg_ref,
                      wig_ref, big_ref, lam_ref, wout_ref, lng_ref, lnb_ref, o_ref,
                      xbuf_scr, hc_scr, z_scr):
    tm = TM_MIX
    i = pl.program_id(1)

    @pl.when(i == 0)
    def _():
        xbuf_scr[0:CONV_HALO, :] = jnp.zeros((CONV_HALO, LRU_WIDTH), F32)
        hc_scr[...] = jnp.zeros_like(hc_scr)

    xbuf_scr[CONV_HALO:CONV_HALO + tm, :] = proj_ref[:, LRU_WIDTH:2 * LRU_WIDTH]
    pos = lax.broadcasted_iota(jnp.int32, (tm, 1), 0) + i * tm

    sub8 = lax.broadcasted_iota(jnp.int32, (tm, LRU_HD), 0) & 7

    for hb in range(LRU_HEADS):
        c0 = hb * LRU_HD
        xz = xbuf_scr[0:CONV_HALO + tm, c0:c0 + LRU_HD]
        xc = convb_ref[:, c0:c0 + LRU_HD] + convw_ref[CONV_W - 1:CONV_W, c0:c0 + LRU_HD] * xz[CONV_HALO:, :]
        for j in range(1, CONV_W):
            kk = CONV_W - 1 - j
            xc = xc + convw_ref[kk:kk + 1, c0:c0 + LRU_HD] * pltpu.roll(xz, j, 0)[CONV_HALO:, :]
        xcb = xc.astype(BF16)
        rg = _sigmoid(_dot(xcb, wrg_ref[hb]) + brg_ref[:, c0:c0 + LRU_HD])
        ig = _sigmoid(_dot(xcb, wig_ref[hb]) + big_ref[:, c0:c0 + LRU_HD])
        log_a = (-LRU_C) * rg * _softplus(-lam_ref[:, c0:c0 + LRU_HD])
        a = jnp.exp(log_a)
        th = jnp.tanh(log_a)
        mult = jnp.sqrt(-2.0 * th / (1.0 - th))
        mult = jnp.where(pos == 0, 1.0, mult)
        u = (xc * ig) * mult

        for d in (1, 2, 4):
            keep = sub8 >= d
            u = a * jnp.where(keep, pltpu.roll(u, d, 0), 0.0) + u
            a = a * jnp.where(keep, pltpu.roll(a, d, 0), 1.0)
        carry = hc_scr[:, c0:c0 + LRU_HD]
        blocks = []
        for blk in range(tm // 8):
            hb8 = a[blk * 8:(blk + 1) * 8, :] * carry + u[blk * 8:(blk + 1) * 8, :]
            carry = hb8[7:8, :]
            blocks.append(hb8)
        hseq = jnp.concatenate(blocks, axis=0)
        hc_scr[:, c0:c0 + LRU_HD] = carry
        gate = proj_ref[:, c0:c0 + LRU_HD]
        z_scr[:, c0:c0 + LRU_HD] = (_gelu_tanh(gate) * hseq).astype(BF16)

    xbuf_scr[0:CONV_HALO, :] = xbuf_scr[tm:tm + CONV_HALO, :]
    y = _dot(z_scr[...], wout_ref[...])
    m = mod_ref[0]
    o_ref[...] = _ln_residual(x_ref[...], y, m[2:3, :], lng_ref[...], lnb_ref[...])


def _odd_mixer(x2, proj, modl, convw, convb, wrg, brg, wig, big, lam, wout, lng, lnb):
    nt = SEQ // TM_MIX
    tok = lambda b, i: (b * nt + i, 0)
    const2 = lambda b, i: (0, 0)
    const3 = lambda b, i: (0, 0, 0)
    return pl.pallas_call(
        _odd_mixer_kernel,
        grid=(BATCH, nt),
        in_specs=[
            pl.BlockSpec((TM_MIX, D_MODEL), tok),
            pl.BlockSpec((TM_MIX, 2 * LRU_WIDTH), tok),
            pl.BlockSpec((1, 6, D_MODEL), lambda b, i: (b, 0, 0)),
            pl.BlockSpec((CONV_W, LRU_WIDTH), const2),
            pl.BlockSpec((1, LRU_WIDTH), const2),
            pl.BlockSpec((LRU_HEADS, LRU_HD, LRU_HD), const3),
            pl.BlockSpec((1, LRU_WIDTH), const2),
            pl.BlockSpec((LRU_HEADS, LRU_HD, LRU_HD), const3),
            pl.BlockSpec((1, LRU_WIDTH), const2),
            pl.BlockSpec((1, LRU_WIDTH), const2),
            pl.BlockSpec((LRU_WIDTH, D_MODEL), const2),
            pl.BlockSpec((1, D_MODEL), const2),
            pl.BlockSpec((1, D_MODEL), const2),
        ],
        out_specs=pl.BlockSpec((TM_MIX, D_MODEL), tok),
        out_shape=jax.ShapeDtypeStruct((TOKENS, D_MODEL), F32),
        scratch_shapes=[
            pltpu.VMEM((CONV_HALO + TM_MIX, LRU_WIDTH), F32),
            pltpu.VMEM((1, LRU_WIDTH), F32),
            pltpu.VMEM((TM_MIX, LRU_WIDTH), BF16),
        ],
        compiler_params=_params("arbitrary", "arbitrary"),
        name="odd_mixer",
    )(x2, proj, modl, convw, convb, wrg, brg, wig, big, lam, wout, lng, lnb)


def _first_max4(v):
    m = jnp.maximum(jnp.maximum(v[0], v[1]), jnp.maximum(v[2], v[3]))
    idx = jnp.where(v[0] == m, 0, jnp.where(v[1] == m, 1, jnp.where(v[2] == m, 2, 3)))
    return m, idx


def _router_kernel(x_ref, mod_ref, wrt_ref, bias_ref, hf_ref, eidx_ref, wts_ref):
    m = mod_ref[0]
    h = x_ref[...] * (1.0 + m[4:5, :]) + m[3:4, :]
    hf_ref[...] = h

    h_hi = h.astype(BF16)
    h_lo = (h - h_hi.astype(F32)).astype(BF16)
    w = wrt_ref[...]
    w_hi = w.astype(BF16)
    w_lo = (w - w_hi.astype(F32)).astype(BF16)
    dg = functools.partial(lax.dot_general, dimension_numbers=_NT, preferred_element_type=F32)
    logits = dg(w_hi, h_hi) + dg(w_hi, h_lo) + dg(w_lo, h_hi)
    score = _sigmoid(logits)
    sel = score + bias_ref[...]

    neg = -jnp.inf
    s_rows = [sel[e:e + 1, :] for e in range(N_EXPERTS)]
    group_score, first, second = [], [], []
    for g in range(N_GROUPS):
        v = s_rows[g * EXP_PER_GROUP:(g + 1) * EXP_PER_GROUP]
        m1, i1 = _first_max4(v)
        v2 = [jnp.where(i1 == j, neg, v[j]) for j in range(EXP_PER_GROUP)]
        m2, i2 = _first_max4(v2)
        group_score.append(m1 + m2)
        first.append(i1)
        second.append(i2)
    _, gbest = _first_max4(group_score)
    pick = lambda rows: jnp.where(gbest == 0, rows[0], jnp.where(gbest == 1, rows[1],
                                  jnp.where(gbest == 2, rows[2], rows[3])))
    e1 = gbest * EXP_PER_GROUP + pick(first)
    e2 = gbest * EXP_PER_GROUP + pick(second)
    s1 = jnp.zeros_like(s_rows[0])
    s2 = jnp.zeros_like(s_rows[0])
    for e in range(N_EXPERTS):
        row = score[e:e + 1, :]
        s1 = s1 + jnp.where(e1 == e, row, 0.0)
        s2 = s2 + jnp.where(e2 == e, row, 0.0)
    den = s1 + s2
    eidx_ref[0:1, :] = e1
    eidx_ref[1:2, :] = e2
    wts_ref[0:1, :] = s1 / den
    wts_ref[1:2, :] = s2 / den


def _router(x2, modl, wrt, bias_col):
    nt = SEQ // TM_ROUTE
    tok = lambda b, i: (b * nt + i, 0)
    tokT = lambda b, i: (0, b * nt + i)
    return pl.pallas_call(
        _router_kernel,
        grid=(BATCH, nt),
        in_specs=[
            pl.BlockSpec((TM_ROUTE, D_MODEL), tok),
            pl.BlockSpec((1, 6, D_MODEL), lambda b, i: (b, 0, 0)),
            pl.BlockSpec((N_EXPERTS, D_MODEL), lambda b, i: (0, 0)),
            pl.BlockSpec((N_EXPERTS, 1), lambda b, i: (0, 0)),
        ],
        out_specs=[
            pl.BlockSpec((TM_ROUTE, D_MODEL), tok),
            pl.BlockSpec((2, TM_ROUTE), tokT),
            pl.BlockSpec((2, TM_ROUTE), tokT),
        ],
        out_shape=[
            jax.ShapeDtypeStruct((TOKENS, D_MODEL), F32),
            jax.ShapeDtypeStruct((2, TOKENS), jnp.int32),
            jax.ShapeDtypeStruct((2, TOKENS), F32),
        ],
        compiler_params=_params("arbitrary", "arbitrary"),
        name="moe_router",
    )(x2, modl, wrt, bias_col)


def _prefix_experts(v):
    eidx = lax.broadcasted_iota(jnp.int32, v.shape, 0)
    d = 1
    while d < N_EXPERTS:
        v = v + jnp.where(eidx >= d, pltpu.roll(v, d, 0), 0.0)
        d *= 2
    return v


def _plan_kernel(e_ref, pos_ref, meta_ref, cnt_scr, run_scr, offs_scr):
    tq = TM_ROUTE
    p = pl.program_id(0)
    i = pl.program_id(1)
    e = e_ref[...]
    eid = lax.broadcasted_iota(jnp.int32, (N_EXPERTS, tq), 0)
    oh0 = jnp.where(eid == e[0:1, :], 1.0, 0.0)
    oh1 = jnp.where(eid == e[1:2, :], 1.0, 0.0)

    @pl.when(p == 0)
    def _():
        @pl.when(i == 0)
        def _():
            cnt_scr[...] = jnp.zeros_like(cnt_scr)

        cnt_scr[...] = cnt_scr[...] + jnp.sum(oh0 + oh1, axis=1, keepdims=True)

    @pl.when((p == 1) & (i == 0))
    def _():
        cnt = cnt_scr[...]
        offs = _prefix_experts(cnt) - cnt
        offs_scr[...] = offs
        run_scr[...] = jnp.zeros_like(run_scr)
        nseg = jnp.zeros_like(cnt)
        for k in range(MAX_SEG_PER_EXPERT):
            nseg = nseg + jnp.where(cnt > float(k * SEG_CAP), 1.0, 0.0)
        seg_end = _prefix_experts(nseg)
        seg_begin = seg_end - nseg
        n_seg = jnp.max(seg_end, axis=0, keepdims=True)
        s = lax.broadcasted_iota(jnp.int32, cnt.shape, 1).astype(F32)
        expert = jnp.sum(jnp.where(seg_end <= jnp.minimum(s, n_seg - 1.0), 1.0, 0.0), axis=0, keepdims=True)
        member = jnp.where(seg_begin <= s, jnp.where(s < seg_end, 1.0, 0.0), 0.0)
        done = (s - seg_begin) * float(SEG_CAP)
        rows = jnp.sum(member * jnp.minimum(float(SEG_CAP), cnt - done), axis=0, keepdims=True)
        start = jnp.sum(member * (offs + done), axis=0, keepdims=True)
        meta_ref[...] = jnp.zeros_like(meta_ref)
        meta_ref[0:1, :] = expert.astype(jnp.int32)
        meta_ref[1:2, :] = start.astype(jnp.int32)
        meta_ref[2:3, :] = rows.astype(jnp.int32)
        meta_ref[3:4, :] = n_seg.astype(jnp.int32)

    @pl.when(p == 1)
    def _():
        rj = lax.broadcasted_iota(jnp.int32, (tq, tq), 0)
        ct = lax.broadcasted_iota(jnp.int32, (tq, tq), 1)
        upper = jnp.where(rj <= ct, 1.0, 0.0).astype(BF16)
        pre0 = _dot(oh0.astype(BF16), upper)
        pre1 = _dot(oh1.astype(BF16), upper)
        tot0 = pre0[:, tq - 1:tq]
        tot1 = pre1[:, tq - 1:tq]
        base = offs_scr[:, 0:1] + run_scr[:, 0:1]
        r0 = jnp.sum(oh0 * (base + pre0 - 1.0), axis=0, keepdims=True)
        r1 = jnp.sum(oh1 * (base + tot0 + pre1 - 1.0), axis=0, keepdims=True)
        pos_ref[0:1, :] = r0.astype(jnp.int32)
        pos_ref[1:2, :] = r1.astype(jnp.int32)
        run_scr[...] = run_scr[...] + (tot0 + tot1)


def _plan(eidx):
    nt = TOKENS // TM_ROUTE
    acc = pltpu.VMEM((N_EXPERTS, LANES), F32)
    return pl.pallas_call(
        _plan_kernel,
        grid=(2, nt),
        in_specs=[pl.BlockSpec((2, TM_ROUTE), lambda p, i: (0, i))],
        out_specs=[
            pl.BlockSpec((2, TM_ROUTE), lambda p, i: (0, i * p)),
            pl.BlockSpec((8, LANES), lambda p, i: (0, 0)),
        ],
        out_shape=[
            jax.ShapeDtypeStruct((2, TOKENS), jnp.int32),
            jax.ShapeDtypeStruct((8, LANES), jnp.int32),
        ],
        scratch_shapes=[acc, acc, acc],
        compiler_params=_params("arbitrary", "arbitrary"),
        name="moe_plan",
    )(eidx)


def _invert_kernel(pos_ref, inv_ref):
    base = pl.program_id(0) * INV_CHUNK

    @pl.when(pl.program_id(0) == 0)
    def _():
        for k in range(INV_SPARE):
            inv_ref[2 * TOKENS + k] = 0

    def body(k, carry):
        inv_ref[pos_ref[k]] = base + k
        return carry

    lax.fori_loop(0, INV_CHUNK, body, 0, unroll=8)


def _invert(pos_flat):
    return pl.pallas_call(
        _invert_kernel,
        grid=(2 * TOKENS // INV_CHUNK,),
        in_specs=[pl.BlockSpec((INV_CHUNK,), lambda i: (i,), memory_space=pltpu.SMEM)],
        out_specs=pl.BlockSpec(memory_space=pltpu.SMEM),
        out_shape=jax.ShapeDtypeStruct((2 * TOKENS + INV_SPARE,), jnp.int32),
        compiler_params=_params("arbitrary"),
        name="moe_invert",
    )(pos_flat)


def _row(ref3, row):
    return ref3.at[row >> 3, pl.ds(row & 7, 1), :]


def _wait_rows(n, max_rows, src3, dst3, sem):
    p = 1
    while p <= max_rows:
        @pl.when((n & p) != 0)
        def _(p=p):
            if p < 8:
                cp = pltpu.make_async_copy(src3.at[0, pl.ds(0, p), :], dst3.at[0, pl.ds(0, p), :], sem)
            else:
                cp = pltpu.make_async_copy(src3.at[pl.ds(0, p // 8)], dst3.at[pl.ds(0, p // 8)], sem)
            cp.wait()

        p *= 2


def _ffn_kernel(se_ref, ss_ref, sr_ref, ns_ref, inv_ref, hf_hbm, wg_hbm, wu_hbm, wd_hbm, out_hbm,
                stag, wbf, xbuf, act_scr, ybuf, sem_w, sem_x, sem_y, *, layer):
    s = pl.program_id(0)
    ph = pl.program_id(1)
    n_seg = ns_ref[0]
    e = se_ref[s]
    start = ss_ref[s]
    rows = sr_ref[s]
    n_sub = (rows + (SEG_SUB - 1)) // SEG_SUB
    half = D_MODEL // 2
    hh = D_EXPERT // 2

    def weight_copies(phase, expert, slot):
        if phase < 2:
            cols = pl.ds(phase * hh, hh)
            return [pltpu.make_async_copy(w_hbm.at[layer, expert, :, cols],
                                          stag.at[slot, :, pl.ds(c * hh, hh)], sem_w.at[slot])
                    for c, w_hbm in enumerate((wg_hbm, wu_hbm))]
        return [pltpu.make_async_copy(wd_hbm.at[layer, expert, :, pl.ds(c * half, half)],
                                      stag.at[slot, pl.ds(c * D_EXPERT, D_EXPERT), :], sem_w.at[slot])
                for c in range(2)]

    def gather_start(seg, xslot):
        seg_start = ss_ref[seg]

        def group(i, carry):
            for j in range(8):
                tok = inv_ref[seg_start + i * 8 + j] & (TOKENS - 1)
                pltpu.make_async_copy(_row(hf_hbm, tok), xbuf.at[xslot, i, pl.ds(j, 1), :],
                                      sem_x.at[xslot]).start()
            return carry

        lax.fori_loop(0, (sr_ref[seg] + 7) >> 3, group, 0)

    def load_weights(phase, slot):
        for cp in weight_copies(phase, e, slot):
            cp.wait()
        blk = D_MODEL // 8
        for kk in range(8):
            wbf[kk * blk:(kk + 1) * blk, :] = stag[slot, kk * blk:(kk + 1) * blk, :].astype(BF16)

    def gate_up_half(phase):
        def sub(k, carry):
            r0 = pl.multiple_of(k * SEG_SUB, SEG_SUB)
            x = xbuf[xslot, pl.ds(k * (SEG_SUB // 8), SEG_SUB // 8)].reshape(SEG_SUB, D_MODEL)
            gu = _dot(x.astype(BF16), wbf[...])
            g = gu[:, 0:hh]
            act_scr[pl.ds(r0, SEG_SUB), phase * hh:(phase + 1) * hh] = (
                (g * _sigmoid(g)) * gu[:, hh:2 * hh]).astype(BF16)
            return carry

        lax.fori_loop(0, n_sub, sub, 0)

    valid = s < n_seg
    q = 3 * s + ph
    slot = q % 2
    xslot = s % 2

    @pl.when(valid & (ph == 0))
    def _():
        @pl.when(s == 0)
        def _():
            xbuf[...] = jnp.zeros_like(xbuf)
            for cp in weight_copies(0, e, slot):
                cp.start()
            gather_start(0, 0)

        for cp in weight_copies(1, e, 1 - slot):
            cp.start()
        _wait_rows(((rows + 7) >> 3) << 3, SEG_CAP, hf_hbm, xbuf.at[xslot], sem_x.at[xslot])

        @pl.when(s + 1 < n_seg)
        def _():
            gather_start(s + 1, 1 - xslot)

        load_weights(0, slot)
        gate_up_half(0)

    @pl.when(valid & (ph == 1))
    def _():
        for cp in weight_copies(2, e, 1 - slot):
            cp.start()
        load_weights(1, slot)
        gate_up_half(1)

    @pl.when(valid & (ph == 2))
    def _():
        @pl.when(s + 1 < n_seg)
        def _():
            for cp in weight_copies(0, se_ref[s + 1], 1 - slot):
                cp.start()

        load_weights(2, slot)

        def sub(k, carry):
            r0 = pl.multiple_of(k * SEG_SUB, SEG_SUB)
            yslot = k % 2

            @pl.when(k >= 2)
            def _():
                _wait_rows(SEG_SUB, SEG_SUB, ybuf.at[yslot], out_hbm, sem_y.at[yslot])

            act = act_scr[pl.ds(r0, SEG_SUB), :]
            for c in range(2):
                ybuf[yslot, :, :, c * half:(c + 1) * half] = _dot(
                    act, wbf[c * D_EXPERT:(c + 1) * D_EXPERT, :]).reshape(SEG_SUB // 8, 8, half)

            n_rows = jnp.minimum(SEG_SUB, rows - r0)
            base = start + r0

            def group(i, c):
                for j in range(8):
                    pltpu.make_async_copy(ybuf.at[yslot, i, pl.ds(j, 1), :],
                                          _row(out_hbm, inv_ref[base + i * 8 + j]),
                                          sem_y.at[yslot]).start(priority=1)
                return c

            lax.fori_loop(0, n_rows >> 3, group, 0)

            def single(r, c):
                pltpu.make_async_copy(_row(ybuf.at[yslot], r), _row(out_hbm, inv_ref[base + r]),
                                      sem_y.at[yslot]).start(priority=1)
                return c

            lax.fori_loop((n_rows >> 3) << 3, n_rows, single, 0)
            return carry

        lax.fori_loop(0, n_sub, sub, 0)
        last = n_sub - 1
        _wait_rows(rows - last * SEG_SUB, SEG_SUB, ybuf.at[last % 2], out_hbm, sem_y.at[last % 2])

        @pl.when(n_sub >= 2)
        def _():
            _wait_rows(SEG_SUB, SEG_SUB, ybuf.at[n_sub % 2], out_hbm, sem_y.at[n_sub % 2])


def _ffn(layer, seg_expert, seg_start, seg_rows, n_seg, inv, hf, wg, wu, wd):
    assert D_EXPERT * 2 == D_MODEL
    anyspec = pl.BlockSpec(memory_space=pl.ANY)
    grid_spec = pltpu.PrefetchScalarGridSpec(
        num_scalar_prefetch=5,
        grid=(N_SEG, 3),
        in_specs=[anyspec, anyspec, anyspec, anyspec],
        out_specs=anyspec,
        scratch_shapes=[
            pltpu.VMEM((2, D_MODEL, D_EXPERT), F32),
            pltpu.VMEM((D_MODEL, D_EXPERT), BF16),
            pltpu.VMEM((2, SEG_CAP // 8, 8, D_MODEL), F32),
            pltpu.VMEM((SEG_CAP, D_EXPERT), BF16),
            pltpu.VMEM((2, SEG_SUB // 8, 8, D_MODEL), F32),
            pltpu.SemaphoreType.DMA((2,)),
            pltpu.SemaphoreType.DMA((2,)),
            pltpu.SemaphoreType.DMA((2,)),
        ],
    )
    y3 = pl.pallas_call(
        functools.partial(_ffn_kernel, layer=layer),
        grid_spec=grid_spec,
        out_shape=jax.ShapeDtypeStruct((2 * TOKENS // 8, 8, D_MODEL), F32),
        compiler_params=_params("arbitrary", "arbitrary"),
        name="moe_ffn",
    )(seg_expert, seg_start, seg_rows, n_seg, inv, hf.reshape(TOKENS // 8, 8, D_MODEL), wg, wu, wd)
    return y3.reshape(2 * TOKENS, D_MODEL)


def _combine_kernel(x_ref, mod_ref, w_ref, lng_ref, lnb_ref, y0_ref, y1_ref, o_ref):
    w = w_ref[...]
    y = w[:, 0:1] * y0_ref[...] + w[:, 1:2] * y1_ref[...]
    m = mod_ref[0]
    o_ref[...] = _ln_residual(x_ref[...], y, m[5:6, :], lng_ref[...], lnb_ref[...])


def _combine(x2, modl, wts_col, lng, lnb, y2):
    nt = SEQ // TM_COMB
    tok = lambda b, i: (b * nt + i, 0)
    tok1 = lambda b, i: (TOKENS // TM_COMB + b * nt + i, 0)
    return pl.pallas_call(
        _combine_kernel,
        grid=(BATCH, nt),
        in_specs=[
            pl.BlockSpec((TM_COMB, D_MODEL), tok),
            pl.BlockSpec((1, 6, D_MODEL), lambda b, i: (b, 0, 0)),
            pl.BlockSpec((TM_COMB, 2), tok),
            pl.BlockSpec((1, D_MODEL), lambda b, i: (0, 0)),
            pl.BlockSpec((1, D_MODEL), lambda b, i: (0, 0)),
            pl.BlockSpec((TM_COMB, D_MODEL), tok),
            pl.BlockSpec((TM_COMB, D_MODEL), tok1),
        ],
        out_specs=pl.BlockSpec((TM_COMB, D_MODEL), tok),
        out_shape=jax.ShapeDtypeStruct((TOKENS, D_MODEL), F32),
        compiler_params=_params("arbitrary", "arbitrary"),
        name="moe_combine",
    )(x2, modl, wts_col, lng, lnb, y2, y2)


def _moe(layer, x2, modl, wrt, bias_col, wg, wu, wd, lng, lnb):
    hf, eidx, wts = _router(x2, modl, wrt, bias_col)
    pos, meta = _plan(eidx)
    inv = _invert(pos.reshape(-1))
    y2 = _ffn(layer, meta[0, :N_SEG], meta[1, :N_SEG], meta[2, :N_SEG], meta[3, :1], inv, hf, wg, wu, wd)
    return _combine(x2, modl, wts.T, lng, lnb, y2)


def kernel(x, c, w_ada, b_ada, ln_mix_g, ln_mix_b, ln_ffn_g, ln_ffn_b, even_w_in, gla_w_glr, gla_b_glr, gla_norm_g, pool_w, pool_scale, even_w_out, odd_w_in, lru_conv_w, lru_conv_b, lru_w_rg, lru_b_rg, lru_w_ig, lru_b_ig, lru_lambda, odd_w_out, w_router, router_bias, exp_w_gate, exp_w_up, exp_w_down):
    mod = _adaln_mod(c, w_ada, b_ada)
    x2 = x.reshape(TOKENS, D_MODEL)
    wrt = w_router.T
    bias_col = router_bias.reshape(N_EXPERTS, 1)
    row = lambda a: a.reshape(1, -1)
    for l in range(DEPTH):
        j = l // 2
        modl = mod[l]
        if l % 2 == 0:
            w_in = even_w_in[j]
            w_main = jnp.concatenate([w_in[:, :SPLIT_R], w_in[:, SPLIT_G:]], axis=1).astype(BF16)
            wgin = jnp.pad(w_in[:, SPLIT_R:SPLIT_G], ((0, 0), (0, LANES - GLA_RANK))).astype(BF16)
            wglr = jnp.pad(gla_w_glr[j], ((0, LANES - GLA_RANK), (0, 0))).astype(BF16)
            proj = _inproj(x2, modl, w_main)
            x2 = _even_mixer(x2, proj, modl, wgin, wglr, row(gla_b_glr[j]), row(gla_norm_g[j]),
                             pool_w[j].astype(BF16), row(pool_scale[j]), even_w_out[j].astype(BF16),
                             row(ln_mix_g[l]), row(ln_mix_b[l]))
        else:
            proj = _inproj(x2, modl, odd_w_in[j].astype(BF16))
            x2 = _odd_mixer(x2, proj, modl, lru_conv_w[j], row(lru_conv_b[j]),
                            lru_w_rg[j].astype(BF16), row(lru_b_rg[j]), lru_w_ig[j].astype(BF16),
                            row(lru_b_ig[j]), row(lru_lambda[j]), odd_w_out[j].astype(BF16),
                            row(ln_mix_g[l]), row(ln_mix_b[l]))
        x2 = _moe(l, x2, modl, wrt, bias_col, exp_w_gate, exp_w_up, exp_w_down,
                  row(ln_ffn_g[l]), row(ln_ffn_b[l]))
    return x2.reshape(BATCH, SEQ, D_MODEL)
```

```python
import functools

import jax
import jax.numpy as jnp
from jax import lax
from jax.experimental import pallas as pl
from jax.experimental.pallas import tpu as pltpu

F32 = jnp.float32
BF16 = jnp.bfloat16

D_MODEL = 2048
BATCH = 2
SEQ = 4096
TOKENS = BATCH * SEQ
DEPTH = 4
ALPHA = (2 * DEPTH) ** 0.25
LN_EPS = 1e-5

GLA_HEADS = 4
GLA_DK = D_MODEL // 4
GLA_DV = D_MODEL // 2
GLA_HK = GLA_DK // GLA_HEADS
GLA_HV = GLA_DV // GLA_HEADS
GLA_RANK = 16
GLA_TAU = 16.0
GLA_CHUNK = 64
POOL_WIDTH = D_MODEL // 2
POOL_WINDOWS = (2, 4, 8, 16)
POOL_GC = POOL_WIDTH // len(POOL_WINDOWS)
POOL_HALO = 16
SPLIT_R = 2 * GLA_DK + 2 * GLA_DV
SPLIT_G = SPLIT_R + GLA_RANK
EVEN_PROJ = SPLIT_R + POOL_WIDTH

LRU_WIDTH = 5 * D_MODEL // 4
LRU_HEADS = 10
LRU_HD = LRU_WIDTH // LRU_HEADS
CONV_W = 4
CONV_HALO = 8
LRU_C = 8.0

N_EXPERTS = 16
N_GROUPS = 4
EXP_PER_GROUP = N_EXPERTS // N_GROUPS
D_EXPERT = D_MODEL // 2

LANES = 128
VMEM_LIMIT = 56 * 1024 * 1024

TM_PROJ = 1024
TN_PROJ = 1024
TM_MIX = 256
TM_EVEN = 128
TM_ROUTE = 512
TM_COMB = 256
SEG_SUB_LOG2 = 8
SEG_SUB = 1 << SEG_SUB_LOG2
GATHER_CHUNK = 128
CHUNKS_PER_SUB = SEG_SUB // GATHER_CHUNK
SEG_CAP = 5 * SEG_SUB
MAX_SEG_PER_EXPERT = -(-TOKENS // SEG_CAP)
N_SEG = N_EXPERTS + (2 * TOKENS) // SEG_CAP + 1
INV_CHUNK = 2048
INV_SPARE = SEG_SUB

_NT = (((1,), (1,)), ((), ()))


def _sigmoid(x):
    return 1.0 / (1.0 + jnp.exp(-x))


def _softplus(x):
    return jnp.maximum(x, 0.0) + jnp.log1p(jnp.exp(-jnp.abs(x)))


def _gelu_tanh(x):
    return 0.5 * x * (1.0 + jnp.tanh(0.7978845608028654 * (x + 0.044715 * (x * x * x))))


def _dot(a, b):
    return jnp.dot(a, b, preferred_element_type=F32)


def _ln_residual(x, y, gate, ln_g, ln_b):
    r = ALPHA * x + (1.0 + gate) * y
    mu = jnp.mean(r, axis=-1, keepdims=True)
    d = r - mu
    var = jnp.mean(d * d, axis=-1, keepdims=True)
    return d * lax.rsqrt(var + LN_EPS) * ln_g + ln_b


def _params(*sem):
    return pltpu.CompilerParams(dimension_semantics=sem, vmem_limit_bytes=VMEM_LIMIT)


def _mod_kernel(c_ref, w_ref, b_ref, o_ref):
    c = c_ref[...]
    sc = (c * _sigmoid(c)).astype(BF16)
    o_ref[0] = _dot(sc, w_ref[0].astype(BF16)) + b_ref[0]


def _adaln_mod(c, w_ada, b_ada):
    tn = 1024
    n = 6 * D_MODEL
    c8 = jnp.pad(c, ((0, 8 - BATCH), (0, 0)))
    out = pl.pallas_call(
        _mod_kernel,
        grid=(DEPTH, n // tn),
        in_specs=[
            pl.BlockSpec((8, D_MODEL), lambda l, j: (0, 0)),
            pl.BlockSpec((1, D_MODEL, tn), lambda l, j: (l, 0, j)),
            pl.BlockSpec((1, 1, tn), lambda l, j: (l, 0, j)),
        ],
        out_specs=pl.BlockSpec((1, 8, tn), lambda l, j: (l, 0, j)),
        out_shape=jax.ShapeDtypeStruct((DEPTH, 8, n), F32),
        compiler_params=_params("arbitrary", "arbitrary"),
        name="adaln_mod",
    )(c8, w_ada, b_ada.reshape(DEPTH, 1, n))
    return out[:, :BATCH, :].reshape(DEPTH, BATCH, 6, D_MODEL)


def _inproj_kernel(x_ref, mod_ref, w_ref, o_ref, h_scr):
    @pl.when(pl.program_id(2) == 0)
    def _():
        m = mod_ref[0]
        h_scr[...] = (x_ref[...] * (1.0 + m[1:2, :]) + m[0:1, :]).astype(BF16)

    o_ref[...] = _dot(h_scr[...], w_ref[...])


def _inproj(x2, modl, w_bf16):
    n = w_bf16.shape[1]
    ntm = SEQ // TM_PROJ
    return pl.pallas_call(
        _inproj_kernel,
        grid=(BATCH, ntm, n // TN_PROJ),
        in_specs=[
            pl.BlockSpec((TM_PROJ, D_MODEL), lambda b, i, j: (b * ntm + i, 0)),
            pl.BlockSpec((1, 6, D_MODEL), lambda b, i, j: (b, 0, 0)),
            pl.BlockSpec((D_MODEL, TN_PROJ), lambda b, i, j: (0, j)),
        ],
        out_specs=pl.BlockSpec((TM_PROJ, TN_PROJ), lambda b, i, j: (b * ntm + i, j)),
        out_shape=jax.ShapeDtypeStruct((TOKENS, n), F32),
        scratch_shapes=[pltpu.VMEM((TM_PROJ, D_MODEL), BF16)],
        compiler_params=_params("arbitrary", "arbitrary", "arbitrary"),
        name="inproj",
    )(x2, modl, w_bf16)


def _even_mixer_kernel(x_ref, proj_ref, mod_ref, wgin_ref, wglr_ref, bglr_ref, normg_ref,
                       poolw_ref, pscale_ref, wout_ref, lng_ref, lnb_ref, o_ref,
                       st_scr, ubuf_scr, cat_scr):
    tb = TM_EVEN
    tm = BATCH * tb
    i = pl.program_id(0)

    @pl.when(i == 0)
    def _():
        st_scr[...] = jnp.zeros_like(st_scr)
        ubuf_scr[:, 0:POOL_HALO, :] = jnp.zeros((BATCH, POOL_HALO, POOL_WIDTH), F32)

    xs = [x_ref[b] for b in range(BATCH)]
    ms = [mod_ref[b] for b in range(BATCH)]
    h = jnp.concatenate([(xs[b] * (1.0 + ms[b][1:2, :]) + ms[b][0:1, :]).astype(BF16)
                         for b in range(BATCH)], axis=0)

    glr = _dot(h, wgin_ref[...])
    z = _dot(glr.astype(BF16), wglr_ref[...]) + bglr_ref[...]
    log_a = -_softplus(-z) * (1.0 / GLA_TAU)

    la_hi = log_a.astype(BF16)
    res = log_a - la_hi.astype(F32)
    la_mid = res.astype(BF16)
    la_lo = (res - la_mid.astype(F32)).astype(BF16)
    ri = lax.broadcasted_iota(jnp.int32, (tm, tm), 0)
    ci = lax.broadcasted_iota(jnp.int32, (tm, tm), 1)
    back = ri - ci
    tril = jnp.where(back >= 0, jnp.where(back <= (ri & (GLA_CHUNK - 1)), 1.0, 0.0), 0.0).astype(BF16)
    bcum = _dot(tril, la_hi) + _dot(tril, la_mid) + _dot(tril, la_lo)

    r64 = lax.broadcasted_iota(jnp.int32, (GLA_CHUNK, GLA_CHUNK), 0)
    c64 = lax.broadcasted_iota(jnp.int32, (GLA_CHUNK, GLA_CHUNK), 1)
    causal = c64 <= r64
    tn_dims = (((0,), (0,)), ((), ()))

    for bi in range(BATCH):
        for c in range(tb // GLA_CHUNK):
            r0 = c * GLA_CHUNK
            j0 = bi * tb + r0
            for hh in range(GLA_HEADS):
                k0 = hh * GLA_HK
                v0 = hh * GLA_HV
                b = bcum[j0:j0 + GLA_CHUNK, k0:k0 + GLA_HK]
                b_last = b[GLA_CHUNK - 1:GLA_CHUNK, :]
                q = proj_ref[bi, r0:r0 + GLA_CHUNK, k0:k0 + GLA_HK]
                k = proj_ref[bi, r0:r0 + GLA_CHUNK, GLA_DK + k0:GLA_DK + k0 + GLA_HK]
                v = proj_ref[bi, r0:r0 + GLA_CHUNK, 2 * GLA_DK + v0:2 * GLA_DK + v0 + GLA_HV].astype(BF16)
                q_in = ((q * (GLA_HK ** -0.5)) * jnp.exp(b)).astype(BF16)
                k_in = (k * jnp.exp(-b)).astype(BF16)
                k_out = (k * jnp.exp(b_last - b)).astype(BF16)
                scores = lax.dot_general(q_in, k_in, _NT, preferred_element_type=F32)
                scores = jnp.where(causal, scores, 0.0).astype(BF16)
                st = st_scr[bi, hh]
                o = _dot(scores, v) + lax.dot_general(q_in, st.astype(BF16), _NT, preferred_element_type=F32)
                kv_t = lax.dot_general(v, k_out, tn_dims, preferred_element_type=F32)
                st_scr[bi, hh] = st * jnp.exp(b_last) + kv_t
                msq = jnp.mean(o * o, axis=-1, keepdims=True)
                on = o * lax.rsqrt(msq + LN_EPS) * normg_ref[:, v0:v0 + GLA_HV]
                r = proj_ref[bi, r0:r0 + GLA_CHUNK, SPLIT_R - GLA_DV + v0:SPLIT_R - GLA_DV + v0 + GLA_HV]
                cat_scr[j0:j0 + GLA_CHUNK, v0:v0 + GLA_HV] = (on * (r * _sigmoid(r))).astype(BF16)

    pos = lax.broadcasted_iota(jnp.int32, (tb, 1), 0) + i * tb
    for bi in range(BATCH):
        ubuf_scr[bi, POOL_HALO:POOL_HALO + tb, :] = proj_ref[bi, :, SPLIT_R:SPLIT_R + POOL_WIDTH]
    for g, w in enumerate(POOL_WINDOWS):
        c0 = g * POOL_GC
        cnt = jnp.minimum(pos + 1, w).astype(F32)
        ps = []
        for bi in range(BATCH):
            cur = ubuf_scr[bi, POOL_HALO:POOL_HALO + tb, c0:c0 + POOL_GC]
            acc = cur
            for kk in range(1, w):
                acc = acc + ubuf_scr[bi, POOL_HALO - kk:POOL_HALO - kk + tb, c0:c0 + POOL_GC]
            ps.append((acc / cnt - cur).astype(BF16))
        pp = _dot(jnp.concatenate(ps, axis=0), poolw_ref[g]) * pscale_ref[:, c0:c0 + POOL_GC]
        cat_scr[:, GLA_DV + c0:GLA_DV + c0 + POOL_GC] = pp.astype(BF16)
    for bi in range(BATCH):
        ubuf_scr[bi, 0:POOL_HALO, :] = ubuf_scr[bi, tb:tb + POOL_HALO, :]

    y = _dot(cat_scr[...], wout_ref[...])
    for bi in range(BATCH):
        o_ref[bi] = _ln_residual(xs[bi], y[bi * tb:(bi + 1) * tb, :], ms[bi][2:3, :], lng_ref[...], lnb_ref[...])


def _even_mixer(x2, proj, modl, wgin, wglr, bglr, normg, poolw, pscale, wout, lng, lnb):
    nt = SEQ // TM_EVEN
    tok = lambda i: (0, i, 0)
    const2 = lambda i: (0, 0)
    const3 = lambda i: (0, 0, 0)
    out = pl.pallas_call(
        _even_mixer_kernel,
        grid=(nt,),
        in_specs=[
            pl.BlockSpec((BATCH, TM_EVEN, D_MODEL), tok),
            pl.BlockSpec((BATCH, TM_EVEN, EVEN_PROJ), tok),
            pl.BlockSpec((BATCH, 6, D_MODEL), const3),
            pl.BlockSpec((D_MODEL, LANES), const2),
            pl.BlockSpec((LANES, GLA_DK), const2),
            pl.BlockSpec((1, GLA_DK), const2),
            pl.BlockSpec((1, GLA_DV), const2),
            pl.BlockSpec((len(POOL_WINDOWS), POOL_GC, POOL_GC), const3),
            pl.BlockSpec((1, POOL_WIDTH), const2),
            pl.BlockSpec((D_MODEL, D_MODEL), const2),
            pl.BlockSpec((1, D_MODEL), const2),
            pl.BlockSpec((1, D_MODEL), const2),
        ],
        out_specs=pl.BlockSpec((BATCH, TM_EVEN, D_MODEL), tok),
        out_shape=jax.ShapeDtypeStruct((BATCH, SEQ, D_MODEL), F32),
        scratch_shapes=[
            pltpu.VMEM((BATCH, GLA_HEADS, GLA_HV, GLA_HK), F32),
            pltpu.VMEM((BATCH, POOL_HALO + TM_EVEN, POOL_WIDTH), F32),
            pltpu.VMEM((BATCH * TM_EVEN, D_MODEL), BF16),
        ],
        compiler_params=_params("arbitrary"),
        name="even_mixer",
    )(x2.reshape(BATCH, SEQ, D_MODEL), proj.reshape(BATCH, SEQ, EVEN_PROJ), modl,
      wgin, wglr, bglr, normg, poolw, pscale, wout, lng, lnb)
    return out.reshape(TOKENS, D_MODEL)


def _odd_mixer_kernel(x_ref, proj_ref, mod_ref, convw_ref, convb_ref, wrg_ref, brg_ref,
                      wig_ref, big_ref, lam_ref, wout_ref, lng_ref, lnb_ref, o_ref,
                      xbuf_scr, hc_scr, z_scr):
    tm = TM_MIX
    i = pl.program_id(1)

    @pl.when(i == 0)
    def _():
        xbuf_scr[0:CONV_HALO, :] = jnp.zeros((CONV_HALO, LRU_WIDTH), F32)
        hc_scr[...] = jnp.zeros_like(hc_scr)

    xbuf_scr[CONV_HALO:CONV_HALO + tm, :] = proj_ref[:, LRU_WIDTH:2 * LRU_WIDTH]
    pos = lax.broadcasted_iota(jnp.int32, (tm, 1), 0) + i * tm

    sub8 = lax.broadcasted_iota(jnp.int32, (tm, LRU_HD), 0) & 7

    for hb in range(LRU_HEADS):
        c0 = hb * LRU_HD
        xz = xbuf_scr[0:CONV_HALO + tm, c0:c0 + LRU_HD]
        xc = convb_ref[:, c0:c0 + LRU_HD] + convw_ref[CONV_W - 1:CONV_W, c0:c0 + LRU_HD] * xz[CONV_HALO:, :]
        for j in range(1, CONV_W):
            kk = CONV_W - 1 - j
            xc = xc + convw_ref[kk:kk + 1, c0:c0 + LRU_HD] * pltpu.roll(xz, j, 0)[CONV_HALO:, :]
        xcb = xc.astype(BF16)
        rg = _sigmoid(_dot(xcb, wrg_ref[hb]) + brg_ref[:, c0:c0 + LRU_HD])
        ig = _sigmoid(_dot(xcb, wig_ref[hb]) + big_ref[:, c0:c0 + LRU_HD])
        log_a = (-LRU_C) * rg * _softplus(-lam_ref[:, c0:c0 + LRU_HD])
        a = jnp.exp(log_a)
        th = jnp.tanh(log_a)
        mult = jnp.sqrt(-2.0 * th / (1.0 - th))
        mult = jnp.where(pos == 0, 1.0, mult)
        u = (xc * ig) * mult

        for d in (1, 2, 4):
            keep = sub8 >= d
            u = a * jnp.where(keep, pltpu.roll(u, d, 0), 0.0) + u
            a = a * jnp.where(keep, pltpu.roll(a, d, 0), 1.0)
        carry = hc_scr[:, c0:c0 + LRU_HD]
        blocks = []
        for blk in range(tm // 8):
            hb8 = a[blk * 8:(blk + 1) * 8, :] * carry + u[blk * 8:(blk + 1) * 8, :]
            carry = hb8[7:8, :]
            blocks.append(hb8)
        hseq = jnp.concatenate(blocks, axis=0)
        hc_scr[:, c0:c0 + LRU_HD] = carry
        gate = proj_ref[:, c0:c0 + LRU_HD]
        z_scr[:, c0:c0 + LRU_HD] = (_gelu_tanh(gate) * hseq).astype(BF16)

    xbuf_scr[0:CONV_HALO, :] = xbuf_scr[tm:tm + CONV_HALO, :]
    y = _dot(z_scr[...], wout_ref[...])
    m = mod_ref[0]
    o_ref[...] = _ln_residual(x_ref[...], y, m[2:3, :], lng_ref[...], lnb_ref[...])


def _odd_mixer(x2, proj, modl, convw, convb, wrg, brg, wig, big, lam, wout, lng, lnb):
    nt = SEQ // TM_MIX
    tok = lambda b, i: (b * nt + i, 0)
    const2 = lambda b, i: (0, 0)
    const3 = lambda b, i: (0, 0, 0)
    return pl.pallas_call(
        _odd_mixer_kernel,
        grid=(BATCH, nt),
        in_specs=[
            pl.BlockSpec((TM_MIX, D_MODEL), tok),
            pl.BlockSpec((TM_MIX, 2 * LRU_WIDTH), tok),
            pl.BlockSpec((1, 6, D_MODEL), lambda b, i: (b, 0, 0)),
            pl.BlockSpec((CONV_W, LRU_WIDTH), const2),
            pl.BlockSpec((1, LRU_WIDTH), const2),
            pl.BlockSpec((LRU_HEADS, LRU_HD, LRU_HD), const3),
            pl.BlockSpec((1, LRU_WIDTH), const2),
            pl.BlockSpec((LRU_HEADS, LRU_HD, LRU_HD), const3),
            pl.BlockSpec((1, LRU_WIDTH), const2),
            pl.BlockSpec((1, LRU_WIDTH), const2),
            pl.BlockSpec((LRU_WIDTH, D_MODEL), const2),
            pl.BlockSpec((1, D_MODEL), const2),
            pl.BlockSpec((1, D_MODEL), const2),
        ],
        out_specs=pl.BlockSpec((TM_MIX, D_MODEL), tok),
        out_shape=jax.ShapeDtypeStruct((TOKENS, D_MODEL), F32),
        scratch_shapes=[
            pltpu.VMEM((CONV_HALO + TM_MIX, LRU_WIDTH), F32),
            pltpu.VMEM((1, LRU_WIDTH), F32),
            pltpu.VMEM((TM_MIX, LRU_WIDTH), BF16),
        ],
        compiler_params=_params("arbitrary", "arbitrary"),
        name="odd_mixer",
    )(x2, proj, modl, convw, convb, wrg, brg, wig, big, lam, wout, lng, lnb)


def _first_max4(v):
    m = jnp.maximum(jnp.maximum(v[0], v[1]), jnp.maximum(v[2], v[3]))
    idx = jnp.where(v[0] == m, 0, jnp.where(v[1] == m, 1, jnp.where(v[2] == m, 2, 3)))
    return m, idx


def _router_kernel(x_ref, mod_ref, wrt_ref, bias_ref, hf_ref, eidx_ref, wts_ref):
    m = mod_ref[0]
    h = x_ref[...] * (1.0 + m[4:5, :]) + m[3:4, :]
    hf_ref[...] = h

    h_hi = h.astype(BF16)
    h_lo = (h - h_hi.astype(F32)).astype(BF16)
    w = wrt_ref[...]
    w_hi = w.astype(BF16)
    w_lo = (w - w_hi.astype(F32)).astype(BF16)
    dg = functools.partial(lax.dot_general, dimension_numbers=_NT, preferred_element_type=F32)
    logits = dg(w_hi, h_hi) + dg(w_hi, h_lo) + dg(w_lo, h_hi)
    score = _sigmoid(logits)
    sel = score + bias_ref[...]

    neg = -jnp.inf
    s_rows = [sel[e:e + 1, :] for e in range(N_EXPERTS)]
    group_score, first, second = [], [], []
    for g in range(N_GROUPS):
        v = s_rows[g * EXP_PER_GROUP:(g + 1) * EXP_PER_GROUP]
        m1, i1 = _first_max4(v)
        v2 = [jnp.where(i1 == j, neg, v[j]) for j in range(EXP_PER_GROUP)]
        m2, i2 = _first_max4(v2)
        group_score.append(m1 + m2)
        first.append(i1)
        second.append(i2)
    _, gbest = _first_max4(group_score)
    pick = lambda rows: jnp.where(gbest == 0, rows[0], jnp.where(gbest == 1, rows[1],
                                  jnp.where(gbest == 2, rows[2], rows[3])))
    e1 = gbest * EXP_PER_GROUP + pick(first)
    e2 = gbest * EXP_PER_GROUP + pick(second)
    s1 = jnp.zeros_like(s_rows[0])
    s2 = jnp.zeros_like(s_rows[0])
    for e in range(N_EXPERTS):
        row = score[e:e + 1, :]
        s1 = s1 + jnp.where(e1 == e, row, 0.0)
        s2 = s2 + jnp.where(e2 == e, row, 0.0)
    den = s1 + s2
    eidx_ref[0:1, :] = e1
    eidx_ref[1:2, :] = e2
    wts_ref[0:1, :] = s1 / den
    wts_ref[1:2, :] = s2 / den


def _router(x2, modl, wrt, bias_col):
    nt = SEQ // TM_ROUTE
    tok = lambda b, i: (b * nt + i, 0)
    tokT = lambda b, i: (0, b * nt + i)
    return pl.pallas_call(
        _router_kernel,
        grid=(BATCH, nt),
        in_specs=[
            pl.BlockSpec((TM_ROUTE, D_MODEL), tok),
            pl.BlockSpec((1, 6, D_MODEL), lambda b, i: (b, 0, 0)),
            pl.BlockSpec((N_EXPERTS, D_MODEL), lambda b, i: (0, 0)),
            pl.BlockSpec((N_EXPERTS, 1), lambda b, i: (0, 0)),
        ],
        out_specs=[
            pl.BlockSpec((TM_ROUTE, D_MODEL), tok),
            pl.BlockSpec((2, TM_ROUTE), tokT),
            pl.BlockSpec((2, TM_ROUTE), tokT),
        ],
        out_shape=[
            jax.ShapeDtypeStruct((TOKENS, D_MODEL), F32),
            jax.ShapeDtypeStruct((2, TOKENS), jnp.int32),
            jax.ShapeDtypeStruct((2, TOKENS), F32),
        ],
        compiler_params=_params("arbitrary", "arbitrary"),
        name="moe_router",
    )(x2, modl, wrt, bias_col)


def _prefix_experts(v):
    eidx = lax.broadcasted_iota(jnp.int32, v.shape, 0)
    d = 1
    while d < N_EXPERTS:
        v = v + jnp.where(eidx >= d, pltpu.roll(v, d, 0), 0.0)
        d *= 2
    return v


def _plan_kernel(e_ref, pos_ref, meta_ref, cnt_scr, run_scr, offs_scr):
    tq = TM_ROUTE
    p = pl.program_id(0)
    i = pl.program_id(1)
    e = e_ref[...]
    eid = lax.broadcasted_iota(jnp.int32, (N_EXPERTS, tq), 0)
    oh0 = jnp.where(eid == e[0:1, :], 1.0, 0.0)
    oh1 = jnp.where(eid == e[1:2, :], 1.0, 0.0)

    @pl.when(p == 0)
    def _():
        @pl.when(i == 0)
        def _():
            cnt_scr[...] = jnp.zeros_like(cnt_scr)

        cnt_scr[...] = cnt_scr[...] + jnp.sum(oh0 + oh1, axis=1, keepdims=True)

    @pl.when((p == 1) & (i == 0))
    def _():
        cnt = cnt_scr[...]
        offs = _prefix_experts(cnt) - cnt
        offs_scr[...] = offs
        run_scr[...] = jnp.zeros_like(run_scr)
        nseg = jnp.zeros_like(cnt)
        for k in range(MAX_SEG_PER_EXPERT):
            nseg = nseg + jnp.where(cnt > float(k * SEG_CAP), 1.0, 0.0)
        seg_end = _prefix_experts(nseg)
        seg_begin = seg_end - nseg
        n_seg = jnp.max(seg_end, axis=0, keepdims=True)
        s = lax.broadcasted_iota(jnp.int32, cnt.shape, 1).astype(F32)
        expert = jnp.sum(jnp.where(seg_end <= jnp.minimum(s, n_seg - 1.0), 1.0, 0.0), axis=0, keepdims=True)
        member = jnp.where(seg_begin <= s, jnp.where(s < seg_end, 1.0, 0.0), 0.0)
        done = (s - seg_begin) * float(SEG_CAP)
        rows = jnp.sum(member * jnp.minimum(float(SEG_CAP), cnt - done), axis=0, keepdims=True)
        start = jnp.sum(member * (offs + done), axis=0, keepdims=True)
        meta_ref[...] = jnp.zeros_like(meta_ref)
        meta_ref[0:1, :] = expert.astype(jnp.int32)
        meta_ref[1:2, :] = start.astype(jnp.int32)
        meta_ref[2:3, :] = rows.astype(jnp.int32)
        meta_ref[3:4, :] = n_seg.astype(jnp.int32)

    @pl.when(p == 1)
    def _():
        rj = lax.broadcasted_iota(jnp.int32, (tq, tq), 0)
        ct = lax.broadcasted_iota(jnp.int32, (tq, tq), 1)
        upper = jnp.where(rj <= ct, 1.0, 0.0).astype(BF16)
        pre0 = _dot(oh0.astype(BF16), upper)
        pre1 = _dot(oh1.astype(BF16), upper)
        tot0 = pre0[:, tq - 1:tq]
        tot1 = pre1[:, tq - 1:tq]
        base = offs_scr[:, 0:1] + run_scr[:, 0:1]
        r0 = jnp.sum(oh0 * (base + pre0 - 1.0), axis=0, keepdims=True)
        r1 = jnp.sum(oh1 * (base + tot0 + pre1 - 1.0), axis=0, keepdims=True)
        pos_ref[0:1, :] = r0.astype(jnp.int32)
        pos_ref[1:2, :] = r1.astype(jnp.int32)
        run_scr[...] = run_scr[...] + (tot0 + tot1)


def _plan(eidx):
    nt = TOKENS // TM_ROUTE
    acc = pltpu.VMEM((N_EXPERTS, LANES), F32)
    return pl.pallas_call(
        _plan_kernel,
        grid=(2, nt),
        in_specs=[pl.BlockSpec((2, TM_ROUTE), lambda p, i: (0, i))],
        out_specs=[
            pl.BlockSpec((2, TM_ROUTE), lambda p, i: (0, i * p)),
            pl.BlockSpec((8, LANES), lambda p, i: (0, 0)),
        ],
        out_shape=[
            jax.ShapeDtypeStruct((2, TOKENS), jnp.int32),
            jax.ShapeDtypeStruct((8, LANES), jnp.int32),
        ],
        scratch_shapes=[acc, acc, acc],
        compiler_params=_params("arbitrary", "arbitrary"),
        name="moe_plan",
    )(eidx)


def _invert_kernel(pos_ref, inv_ref):
    base = pl.program_id(0) * INV_CHUNK

    @pl.when(pl.program_id(0) == 0)
    def _():
        for k in range(INV_SPARE):
            inv_ref[2 * TOKENS + k] = 0

    def body(k, carry):
        inv_ref[pos_ref[k]] = base + k
        return carry

    lax.fori_loop(0, INV_CHUNK, body, 0, unroll=8)


def _invert(pos_flat):
    return pl.pallas_call(
        _invert_kernel,
        grid=(2 * TOKENS // INV_CHUNK,),
        in_specs=[pl.BlockSpec((INV_CHUNK,), lambda i: (i,), memory_space=pltpu.SMEM)],
        out_specs=pl.BlockSpec(memory_space=pltpu.SMEM),
        out_shape=jax.ShapeDtypeStruct((2 * TOKENS + INV_SPARE,), jnp.int32),
        compiler_params=_params("arbitrary"),
        name="moe_invert",
    )(pos_flat)


def _row(ref3, row):
    return ref3.at[row >> 3, pl.ds(row & 7, 1), :]


def _wait_rows(n, max_rows, src3, dst3, sem):
    p = 1
    while p <= max_rows:
        @pl.when((n & p) != 0)
        def _(p=p):
            if p < 8:
                cp = pltpu.make_async_copy(src3.at[0, pl.ds(0, p), :], dst3.at[0, pl.ds(0, p), :], sem)
            else:
                cp = pltpu.make_async_copy(src3.at[pl.ds(0, p // 8)], dst3.at[pl.ds(0, p // 8)], sem)
            cp.wait()

        p *= 2


def _ffn_kernel(se_ref, ss_ref, sr_ref, ns_ref, inv_ref, hf_hbm, wg_hbm, wu_hbm, wd_hbm, out_hbm,
                stag, wbf, xb0, xb1, act_scr, ybuf, sem_w, sem_x, sem_y, *, layer):
    s = pl.program_id(0)
    ph = pl.program_id(1)
    n_seg = ns_ref[0]
    e = se_ref[s]
    start = ss_ref[s]
    rows = sr_ref[s]
    n_sub = (rows + (SEG_SUB - 1)) >> SEG_SUB_LOG2
    half = D_MODEL // 2
    hh = D_EXPERT // 2
    nxt = jnp.minimum(s + 1, N_SEG - 1)
    next_start = ss_ref[nxt]
    next_chunks = jnp.where(s + 1 < n_seg, ((sr_ref[nxt] + (SEG_SUB - 1)) >> SEG_SUB_LOG2) * CHUNKS_PER_SUB, 0)

    def weight_copies(phase, expert, slot):
        if phase < 2:
            cols = pl.ds(phase * hh, hh)
            return [pltpu.make_async_copy(w_hbm.at[layer, expert, :, cols],
                                          stag.at[slot, :, pl.ds(c * hh, hh)], sem_w.at[slot])
                    for c, w_hbm in enumerate((wg_hbm, wu_hbm))]
        return [pltpu.make_async_copy(wd_hbm.at[layer, expert, :, pl.ds(c * half, half)],
                                      stag.at[slot, pl.ds(c * D_EXPERT, D_EXPERT), :], sem_w.at[slot])
                for c in range(2)]

    def gather_chunk(seg_start, c, xdst, sem):
        base = seg_start + c * GATHER_CHUNK
        g0 = c * (GATHER_CHUNK // 8)
        for i in range(GATHER_CHUNK // 8):
            for j in range(8):
                tok = inv_ref[base + (i * 8 + j)] & (TOKENS - 1)
                pltpu.make_async_copy(hf_hbm.at[pl.ds(tok, 1), :], xdst.at[g0 + i, pl.ds(j, 1), :], sem).start()

    def load_weights(phase, slot):
        for cp in weight_copies(phase, e, slot):
            cp.wait()
        blk = D_MODEL // 8
        for kk in range(8):
            wbf[kk * blk:(kk + 1) * blk, :] = stag[slot, kk * blk:(kk + 1) * blk, :].astype(BF16)

    def gate_up_half(phase, xcur, xnext, sem_next):
        first_chunk = phase * n_sub
        with_gather = jnp.clip(next_chunks - first_chunk, 0, n_sub)

        def block(k, gather):
            r0 = pl.multiple_of(k * SEG_SUB, SEG_SUB)
            x = xcur[pl.ds(k * (SEG_SUB // 8), SEG_SUB // 8)].reshape(SEG_SUB, D_MODEL)
            gu = _dot(x.astype(BF16), wbf[...])
            g = gu[:, 0:hh]
            act_scr[pl.ds(r0, SEG_SUB), phase * hh:(phase + 1) * hh] = (
                (g * _sigmoid(g)) * gu[:, hh:2 * hh]).astype(BF16)
            if gather:
                gather_chunk(next_start, first_chunk + k, xnext, sem_next)

        def gathering(k, carry):
            block(k, True)
            return carry

        def plain(k, carry):
            block(k, False)
            return carry

        lax.fori_loop(0, with_gather, gathering, 0)
        lax.fori_loop(with_gather, n_sub, plain, 0)

    def leftover_chunks(seg_start, first, total, xdst, sem):
        def body(c, carry):
            gather_chunk(seg_start, c, xdst, sem)
            return carry

        lax.fori_loop(first, total, body, 0)

    def wait_gathered(xcur, sem):
        _wait_rows(n_sub * SEG_SUB, SEG_CAP, xcur, xcur, sem)

    valid = s < n_seg
    q = 3 * s + ph
    slot = q % 2

    def phase0(xcur, xnext, sem_cur, sem_next):
        @pl.when(s == 0)
        def _():
            xb0[...] = jnp.zeros_like(xb0)
            xb1[...] = jnp.zeros_like(xb1)
            for cp in weight_copies(0, e, slot):
                cp.start()
            leftover_chunks(start, 0, n_sub * CHUNKS_PER_SUB, xcur, sem_cur)

        for cp in weight_copies(1, e, 1 - slot):
            cp.start()
        wait_gathered(xcur, sem_cur)
        load_weights(0, slot)
        gate_up_half(0, xcur, xnext, sem_next)

    def phase1(xcur, xnext, sem_next):
        for cp in weight_copies(2, e, 1 - slot):
            cp.start()
        load_weights(1, slot)
        gate_up_half(1, xcur, xnext, sem_next)
        leftover_chunks(next_start, 2 * n_sub, next_chunks, xnext, sem_next)

    even = (s % 2) == 0

    @pl.when(valid & (ph == 0) & even)
    def _():
        phase0(xb0, xb1, sem_x.at[0], sem_x.at[1])

    @pl.when(valid & (ph == 0) & jnp.logical_not(even))
    def _():
        phase0(xb1, xb0, sem_x.at[1], sem_x.at[0])

    @pl.when(valid & (ph == 1) & even)
    def _():
        phase1(xb0, xb1, sem_x.at[1])

    @pl.when(valid & (ph == 1) & jnp.logical_not(even))
    def _():
        phase1(xb1, xb0, sem_x.at[0])

    @pl.when(valid & (ph == 2))
    def _():
        @pl.when(s + 1 < n_seg)
        def _():
            for cp in weight_copies(0, se_ref[s + 1], 1 - slot):
                cp.start()

        load_weights(2, slot)

        def out_row(pair):
            return out_hbm.at[pl.ds(pair, 1), :]

        def down(k):
            r0 = pl.multiple_of(k * SEG_SUB, SEG_SUB)
            act = act_scr[pl.ds(r0, SEG_SUB), :]
            for c in range(2):
                ybuf[k % 2, :, :, c * half:(c + 1) * half] = _dot(
                    act, wbf[c * D_EXPERT:(c + 1) * D_EXPERT, :]).reshape(SEG_SUB // 8, 8, half)

        def scatter_full(k):
            base = start + k * SEG_SUB
            for i in range(SEG_SUB // 8):
                for j in range(8):
                    pltpu.make_async_copy(ybuf.at[k % 2, i, pl.ds(j, 1), :], out_row(inv_ref[base + (i * 8 + j)]),
                                          sem_y.at[k % 2]).start(priority=1)

        def wait_scattered(n, k):
            _wait_rows(n, SEG_SUB, ybuf.at[k % 2], ybuf.at[k % 2], sem_y.at[k % 2])

        down(0)

        @pl.when(n_sub >= 2)
        def _():
            scatter_full(0)
            down(1)

        def steady(k, carry):
            wait_scattered(SEG_SUB, k)
            scatter_full(k - 1)
            down(k)
            return carry

        lax.fori_loop(2, n_sub, steady, 0)

        last = n_sub - 1
        n_rows = rows - last * SEG_SUB
        base = start + last * SEG_SUB

        def group(i, c):
            for j in range(8):
                pltpu.make_async_copy(ybuf.at[last % 2, i, pl.ds(j, 1), :], out_row(inv_ref[base + i * 8 + j]),
                                      sem_y.at[last % 2]).start(priority=1)
            return c

        lax.fori_loop(0, n_rows >> 3, group, 0)

        def single(r, c):
            pltpu.make_async_copy(_row(ybuf.at[last % 2], r), out_row(inv_ref[base + r]),
                                  sem_y.at[last % 2]).start(priority=1)
            return c

        lax.fori_loop((n_rows >> 3) << 3, n_rows, single, 0)

        @pl.when(n_sub >= 2)
        def _():
            wait_scattered(SEG_SUB, n_sub)
        wait_scattered(n_rows, last)


def _ffn(layer, seg_expert, seg_start, seg_rows, n_seg, inv, hf, wg, wu, wd):
    assert D_EXPERT * 2 == D_MODEL
    anyspec = pl.BlockSpec(memory_space=pl.ANY)
    grid_spec = pltpu.PrefetchScalarGridSpec(
        num_scalar_prefetch=5,
        grid=(N_SEG, 3),
        in_specs=[anyspec, anyspec, anyspec, anyspec],
        out_specs=anyspec,
        scratch_shapes=[
            pltpu.VMEM((2, D_MODEL, D_EXPERT), F32),
            pltpu.VMEM((D_MODEL, D_EXPERT), BF16),
            pltpu.VMEM((SEG_CAP // 8, 8, D_MODEL), F32),
            pltpu.VMEM((SEG_CAP // 8, 8, D_MODEL), F32),
            pltpu.VMEM((SEG_CAP, D_EXPERT), BF16),
            pltpu.VMEM((2, SEG_SUB // 8, 8, D_MODEL), F32),
            pltpu.SemaphoreType.DMA((2,)),
            pltpu.SemaphoreType.DMA((2,)),
            pltpu.SemaphoreType.DMA((2,)),
        ],
    )
    return pl.pallas_call(
        functools.partial(_ffn_kernel, layer=layer),
        grid_spec=grid_spec,
        out_shape=jax.ShapeDtypeStruct((2 * TOKENS, D_MODEL), F32),
        compiler_params=_params("arbitrary", "arbitrary"),
        name="moe_ffn",
    )(seg_expert, seg_start, seg_rows, n_seg, inv, hf, wg, wu, wd)


def _combine_kernel(x_ref, mod_ref, w_ref, lng_ref, lnb_ref, y0_ref, y1_ref, o_ref):
    w = w_ref[...]
    y = w[:, 0:1] * y0_ref[...] + w[:, 1:2] * y1_ref[...]
    m = mod_ref[0]
    o_ref[...] = _ln_residual(x_ref[...], y, m[5:6, :], lng_ref[...], lnb_ref[...])


def _combine(x2, modl, wts_col, lng, lnb, y2):
    nt = SEQ // TM_COMB
    tok = lambda b, i: (b * nt + i, 0)
    tok1 = lambda b, i: (TOKENS // TM_COMB + b * nt + i, 0)
    return pl.pallas_call(
        _combine_kernel,
        grid=(BATCH, nt),
        in_specs=[
            pl.BlockSpec((TM_COMB, D_MODEL), tok),
            pl.BlockSpec((1, 6, D_MODEL), lambda b, i: (b, 0, 0)),
            pl.BlockSpec((TM_COMB, 2), tok),
            pl.BlockSpec((1, D_MODEL), lambda b, i: (0, 0)),
            pl.BlockSpec((1, D_MODEL), lambda b, i: (0, 0)),
            pl.BlockSpec((TM_COMB, D_MODEL), tok),
            pl.BlockSpec((TM_COMB, D_MODEL), tok1),
        ],
        out_specs=pl.BlockSpec((TM_COMB, D_MODEL), tok),
        out_shape=jax.ShapeDtypeStruct((TOKENS, D_MODEL), F32),
        compiler_params=_params("arbitrary", "arbitrary"),
        name="moe_combine",
    )(x2, modl, wts_col, lng, lnb, y2, y2)


def _moe(layer, x2, modl, wrt, bias_col, wg, wu, wd, lng, lnb):
    hf, eidx, wts = _router(x2, modl, wrt, bias_col)
    pos, meta = _plan(eidx)
    inv = _invert(pos.reshape(-1))
    y2 = _ffn(layer, meta[0, :N_SEG], meta[1, :N_SEG], meta[2, :N_SEG], meta[3, :1], inv, hf, wg, wu, wd)
    return _combine(x2, modl, wts.T, lng, lnb, y2)


def kernel(x, c, w_ada, b_ada, ln_mix_g, ln_mix_b, ln_ffn_g, ln_ffn_b, even_w_in, gla_w_glr, gla_b_glr, gla_norm_g, pool_w, pool_scale, even_w_out, odd_w_in, lru_conv_w, lru_conv_b, lru_w_rg, lru_b_rg, lru_w_ig, lru_b_ig, lru_lambda, odd_w_out, w_router, router_bias, exp_w_gate, exp_w_up, exp_w_down):
    mod = _adaln_mod(c, w_ada, b_ada)
    x2 = x.reshape(TOKENS, D_MODEL)
    wrt = w_router.T
    bias_col = router_bias.reshape(N_EXPERTS, 1)
    row = lambda a: a.reshape(1, -1)
    for l in range(DEPTH):
        j = l // 2
        modl = mod[l]
        if l % 2 == 0:
            w_in = even_w_in[j]
            w_main = jnp.concatenate([w_in[:, :SPLIT_R], w_in[:, SPLIT_G:]], axis=1).astype(BF16)
            wgin = jnp.pad(w_in[:, SPLIT_R:SPLIT_G], ((0, 0), (0, LANES - GLA_RANK))).astype(BF16)
            wglr = jnp.pad(gla_w_glr[j], ((0, LANES - GLA_RANK), (0, 0))).astype(BF16)
            proj = _inproj(x2, modl, w_main)
            x2 = _even_mixer(x2, proj, modl, wgin, wglr, row(gla_b_glr[j]), row(gla_norm_g[j]),
                             pool_w[j].astype(BF16), row(pool_scale[j]), even_w_out[j].astype(BF16),
                             row(ln_mix_g[l]), row(ln_mix_b[l]))
        else:
            proj = _inproj(x2, modl, odd_w_in[j].astype(BF16))
            x2 = _odd_mixer(x2, proj, modl, lru_conv_w[j], row(lru_conv_b[j]),
                            lru_w_rg[j].astype(BF16), row(lru_b_rg[j]), lru_w_ig[j].astype(BF16),
                            row(lru_b_ig[j]), row(lru_lambda[j]), odd_w_out[j].astype(BF16),
                            row(ln_mix_g[l]), row(ln_mix_b[l]))
        x2 = _moe(l, x2, modl, wrt, bias_col, exp_w_gate, exp_w_up, exp_w_down,
                  row(ln_ffn_g[l]), row(ln_ffn_b[l]))
    return x2.reshape(BATCH, SEQ, D_MODEL)
```

```python
import functools

import jax
import jax.numpy as jnp
from jax import lax
from jax.experimental import pallas as pl
from jax.experimental.pallas import tpu as pltpu

F32 = jnp.float32
BF16 = jnp.bfloat16

D_MODEL = 2048
BATCH = 2
SEQ = 4096
TOKENS = BATCH * SEQ
DEPTH = 4
ALPHA = (2 * DEPTH) ** 0.25
LN_EPS = 1e-5

GLA_HEADS = 4
GLA_DK = D_MODEL // 4
GLA_DV = D_MODEL // 2
GLA_HK = GLA_DK // GLA_HEADS
GLA_HV = GLA_DV // GLA_HEADS
GLA_RANK = 16
GLA_TAU = 16.0
GLA_CHUNK = 64
POOL_WIDTH = D_MODEL // 2
POOL_WINDOWS = (2, 4, 8, 16)
POOL_GC = POOL_WIDTH // len(POOL_WINDOWS)
POOL_HALO = 16
SPLIT_R = 2 * GLA_DK + 2 * GLA_DV
SPLIT_G = SPLIT_R + GLA_RANK
EVEN_PROJ = SPLIT_R + POOL_WIDTH

LRU_WIDTH = 5 * D_MODEL // 4
LRU_HEADS = 10
LRU_HD = LRU_WIDTH // LRU_HEADS
CONV_W = 4
CONV_HALO = 8
LRU_C = 8.0

N_EXPERTS = 16
N_GROUPS = 4
EXP_PER_GROUP = N_EXPERTS // N_GROUPS
D_EXPERT = D_MODEL // 2

LANES = 128
VMEM_LIMIT = 56 * 1024 * 1024

TM_PROJ = 1024
TN_PROJ = 1024
TM_MIX = 256
TM_EVEN = 128
OUT_SLAB = 256
OUT_SLAB_EVERY = (BATCH * (TM_EVEN // GLA_CHUNK) * GLA_HEADS) // (D_MODEL // OUT_SLAB)
TM_ROUTE = 512
TM_COMB = 256
SEG_SUB_LOG2 = 8
SEG_SUB = 1 << SEG_SUB_LOG2
GATHER_CHUNK = 128
CHUNKS_PER_SUB = SEG_SUB // GATHER_CHUNK
SEG_CAP = 5 * SEG_SUB
MAX_SEG_PER_EXPERT = -(-TOKENS // SEG_CAP)
N_SEG = N_EXPERTS + (2 * TOKENS) // SEG_CAP + 1
W_SLICES = 8
INV_CHUNK = 2048
INV_SPARE = SEG_SUB

_NT = (((1,), (1,)), ((), ()))


def _sigmoid(x):
    return 1.0 / (1.0 + jnp.exp(-x))


def _softplus(x):
    return jnp.maximum(x, 0.0) + jnp.log1p(jnp.exp(-jnp.abs(x)))


def _gelu_tanh(x):
    return 0.5 * x * (1.0 + jnp.tanh(0.7978845608028654 * (x + 0.044715 * (x * x * x))))


def _dot(a, b):
    return jnp.dot(a, b, preferred_element_type=F32)


def _ln_residual(x, y, gate, ln_g, ln_b):
    r = ALPHA * x + (1.0 + gate) * y
    mu = jnp.mean(r, axis=-1, keepdims=True)
    d = r - mu
    var = jnp.mean(d * d, axis=-1, keepdims=True)
    return d * lax.rsqrt(var + LN_EPS) * ln_g + ln_b


def _params(*sem):
    return pltpu.CompilerParams(dimension_semantics=sem, vmem_limit_bytes=VMEM_LIMIT)


def _mod_kernel(c_ref, w_ref, b_ref, o_ref):
    c = c_ref[...]
    sc = (c * _sigmoid(c)).astype(BF16)
    o_ref[0] = _dot(sc, w_ref[0].astype(BF16)) + b_ref[0]


def _adaln_mod(c, w_ada, b_ada):
    tn = 1024
    n = 6 * D_MODEL
    c8 = jnp.pad(c, ((0, 8 - BATCH), (0, 0)))
    out = pl.pallas_call(
        _mod_kernel,
        grid=(DEPTH, n // tn),
        in_specs=[
            pl.BlockSpec((8, D_MODEL), lambda l, j: (0, 0)),
            pl.BlockSpec((1, D_MODEL, tn), lambda l, j: (l, 0, j)),
            pl.BlockSpec((1, 1, tn), lambda l, j: (l, 0, j)),
        ],
        out_specs=pl.BlockSpec((1, 8, tn), lambda l, j: (l, 0, j)),
        out_shape=jax.ShapeDtypeStruct((DEPTH, 8, n), F32),
        compiler_params=_params("arbitrary", "arbitrary"),
        name="adaln_mod",
    )(c8, w_ada, b_ada.reshape(DEPTH, 1, n))
    return out[:, :BATCH, :].reshape(DEPTH, BATCH, 6, D_MODEL)


def _inproj_kernel(x_ref, mod_ref, w_ref, o_ref, h_scr):
    @pl.when(pl.program_id(2) == 0)
    def _():
        m = mod_ref[0]
        h_scr[...] = (x_ref[...] * (1.0 + m[1:2, :]) + m[0:1, :]).astype(BF16)

    o_ref[...] = _dot(h_scr[...], w_ref[...])


def _inproj(x2, modl, w_bf16):
    n = w_bf16.shape[1]
    ntm = SEQ // TM_PROJ
    return pl.pallas_call(
        _inproj_kernel,
        grid=(BATCH, ntm, n // TN_PROJ),
        in_specs=[
            pl.BlockSpec((TM_PROJ, D_MODEL), lambda b, i, j: (b * ntm + i, 0)),
            pl.BlockSpec((1, 6, D_MODEL), lambda b, i, j: (b, 0, 0)),
            pl.BlockSpec((D_MODEL, TN_PROJ), lambda b, i, j: (0, j)),
        ],
        out_specs=pl.BlockSpec((TM_PROJ, TN_PROJ), lambda b, i, j: (b * ntm + i, j)),
        out_shape=jax.ShapeDtypeStruct((TOKENS, n), F32),
        scratch_shapes=[pltpu.VMEM((TM_PROJ, D_MODEL), BF16)],
        compiler_params=_params("arbitrary", "arbitrary", "arbitrary"),
        name="inproj",
    )(x2, modl, w_bf16)


def _even_mixer_kernel(x_ref, xprev_ref, proj_ref, mod_ref, wgin_ref, wglr_ref, bglr_ref, normg_ref,
                       poolw_ref, pscale_ref, wout_ref, lng_ref, lnb_ref, o_ref,
                       st_scr, ubuf_scr, y_scr, cat0_scr, cat1_scr):
    i = pl.program_id(0)

    @pl.when(i == 0)
    def _():
        st_scr[...] = jnp.zeros_like(st_scr)
        ubuf_scr[:, 0:POOL_HALO, :] = jnp.zeros((BATCH, POOL_HALO, POOL_WIDTH), F32)
        cat1_scr[...] = jnp.zeros_like(cat1_scr)

    refs = (x_ref, xprev_ref, proj_ref, mod_ref, wgin_ref, wglr_ref, bglr_ref, normg_ref,
            poolw_ref, pscale_ref, wout_ref, lng_ref, lnb_ref, o_ref, st_scr, ubuf_scr, y_scr)

    @pl.when(i % 2 == 0)
    def _():
        _even_mixer_step(i, *refs, cat_scr=cat0_scr, cat_prev=cat1_scr)

    @pl.when(i % 2 == 1)
    def _():
        _even_mixer_step(i, *refs, cat_scr=cat1_scr, cat_prev=cat0_scr)


def _even_mixer_step(i, x_ref, xprev_ref, proj_ref, mod_ref, wgin_ref, wglr_ref, bglr_ref, normg_ref,
                     poolw_ref, pscale_ref, wout_ref, lng_ref, lnb_ref, o_ref, st_scr, ubuf_scr, y_scr,
                     *, cat_scr, cat_prev):
    tb = TM_EVEN
    tm = BATCH * tb
    ms = [mod_ref[b] for b in range(BATCH)]

    h = jnp.concatenate([(x_ref[b] * (1.0 + ms[b][1:2, :]) + ms[b][0:1, :]).astype(BF16)
                         for b in range(BATCH)], axis=0)

    glr = _dot(h, wgin_ref[...])
    z = _dot(glr.astype(BF16), wglr_ref[...]) + bglr_ref[...]
    log_a = -_softplus(-z) * (1.0 / GLA_TAU)

    la_hi = log_a.astype(BF16)
    res = log_a - la_hi.astype(F32)
    la_mid = res.astype(BF16)
    la_lo = (res - la_mid.astype(F32)).astype(BF16)
    ri = lax.broadcasted_iota(jnp.int32, (tm, tm), 0)
    ci = lax.broadcasted_iota(jnp.int32, (tm, tm), 1)
    back = ri - ci
    tril = jnp.where(back >= 0, jnp.where(back <= (ri & (GLA_CHUNK - 1)), 1.0, 0.0), 0.0).astype(BF16)
    bcum = _dot(tril, la_hi) + _dot(tril, la_mid) + _dot(tril, la_lo)

    r64 = lax.broadcasted_iota(jnp.int32, (GLA_CHUNK, GLA_CHUNK), 0)
    c64 = lax.broadcasted_iota(jnp.int32, (GLA_CHUNK, GLA_CHUNK), 1)
    causal = c64 <= r64
    tn_dims = (((0,), (0,)), ((), ()))

    for bi in range(BATCH):
        for c in range(tb // GLA_CHUNK):
            r0 = c * GLA_CHUNK
            j0 = bi * tb + r0
            for hh in range(GLA_HEADS):
                k0 = hh * GLA_HK
                v0 = hh * GLA_HV
                b = bcum[j0:j0 + GLA_CHUNK, k0:k0 + GLA_HK]
                b_last = b[GLA_CHUNK - 1:GLA_CHUNK, :]
                q = proj_ref[bi, r0:r0 + GLA_CHUNK, k0:k0 + GLA_HK]
                k = proj_ref[bi, r0:r0 + GLA_CHUNK, GLA_DK + k0:GLA_DK + k0 + GLA_HK]
                v = proj_ref[bi, r0:r0 + GLA_CHUNK, 2 * GLA_DK + v0:2 * GLA_DK + v0 + GLA_HV].astype(BF16)
                q_in = ((q * (GLA_HK ** -0.5)) * jnp.exp(b)).astype(BF16)
                k_in = (k * jnp.exp(-b)).astype(BF16)
                k_out = (k * jnp.exp(b_last - b)).astype(BF16)
                scores = lax.dot_general(q_in, k_in, _NT, preferred_element_type=F32)
                scores = jnp.where(causal, scores, 0.0).astype(BF16)
                st = st_scr[bi, hh]
                o = _dot(scores, v) + lax.dot_general(q_in, st.astype(BF16), _NT, preferred_element_type=F32)
                kv_t = lax.dot_general(v, k_out, tn_dims, preferred_element_type=F32)
                st_scr[bi, hh] = st * jnp.exp(b_last) + kv_t
                msq = jnp.mean(o * o, axis=-1, keepdims=True)
                on = o * lax.rsqrt(msq + LN_EPS) * normg_ref[:, v0:v0 + GLA_HV]
                r = proj_ref[bi, r0:r0 + GLA_CHUNK, SPLIT_R - GLA_DV + v0:SPLIT_R - GLA_DV + v0 + GLA_HV]
                cat_scr[j0:j0 + GLA_CHUNK, v0:v0 + GLA_HV] = (on * (r * _sigmoid(r))).astype(BF16)
                idx = (bi * (tb // GLA_CHUNK) + c) * GLA_HEADS + hh
                if idx % OUT_SLAB_EVERY == OUT_SLAB_EVERY - 1:
                    s0 = (idx // OUT_SLAB_EVERY) * OUT_SLAB
                    y_scr[:, s0:s0 + OUT_SLAB] = _dot(cat_prev[...], wout_ref[:, s0:s0 + OUT_SLAB])

    pos = lax.broadcasted_iota(jnp.int32, (tb, 1), 0) + jnp.minimum(i, SEQ // tb - 1) * tb
    for bi in range(BATCH):
        ubuf_scr[bi, POOL_HALO:POOL_HALO + tb, :] = proj_ref[bi, :, SPLIT_R:SPLIT_R + POOL_WIDTH]
    for g, w in enumerate(POOL_WINDOWS):
        c0 = g * POOL_GC
        cnt = jnp.minimum(pos + 1, w).astype(F32)
        ps = []
        for bi in range(BATCH):
            cur = ubuf_scr[bi, POOL_HALO:POOL_HALO + tb, c0:c0 + POOL_GC]
            acc = cur
            for kk in range(1, w):
                acc = acc + ubuf_scr[bi, POOL_HALO - kk:POOL_HALO - kk + tb, c0:c0 + POOL_GC]
            ps.append((acc / cnt - cur).astype(BF16))
        pp = _dot(jnp.concatenate(ps, axis=0), poolw_ref[g]) * pscale_ref[:, c0:c0 + POOL_GC]
        cat_scr[:, GLA_DV + c0:GLA_DV + c0 + POOL_GC] = pp.astype(BF16)
    for bi in range(BATCH):
        ubuf_scr[bi, 0:POOL_HALO, :] = ubuf_scr[bi, tb:tb + POOL_HALO, :]

    for bi in range(BATCH):
        o_ref[bi] = _ln_residual(xprev_ref[bi], y_scr[bi * tb:(bi + 1) * tb, :], ms[bi][2:3, :],
                                 lng_ref[...], lnb_ref[...])


def _even_mixer(x2, proj, modl, wgin, wglr, bglr, normg, poolw, pscale, wout, lng, lnb):
    nt = SEQ // TM_EVEN
    cur = lambda i: (0, jnp.minimum(i, nt - 1), 0)
    prev = lambda i: (0, jnp.maximum(i - 1, 0), 0)
    const2 = lambda i: (0, 0)
    const3 = lambda i: (0, 0, 0)
    x3 = x2.reshape(BATCH, SEQ, D_MODEL)
    cat = pltpu.VMEM((BATCH * TM_EVEN, D_MODEL), BF16)
    out = pl.pallas_call(
        _even_mixer_kernel,
        grid=(nt + 1,),
        in_specs=[
            pl.BlockSpec((BATCH, TM_EVEN, D_MODEL), cur),
            pl.BlockSpec((BATCH, TM_EVEN, D_MODEL), prev),
            pl.BlockSpec((BATCH, TM_EVEN, EVEN_PROJ), cur),
            pl.BlockSpec((BATCH, 6, D_MODEL), const3),
            pl.BlockSpec((D_MODEL, LANES), const2),
            pl.BlockSpec((LANES, GLA_DK), const2),
            pl.BlockSpec((1, GLA_DK), const2),
            pl.BlockSpec((1, GLA_DV), const2),
            pl.BlockSpec((len(POOL_WINDOWS), POOL_GC, POOL_GC), const3),
            pl.BlockSpec((1, POOL_WIDTH), const2),
            pl.BlockSpec((D_MODEL, D_MODEL), const2),
            pl.BlockSpec((1, D_MODEL), const2),
            pl.BlockSpec((1, D_MODEL), const2),
        ],
        out_specs=pl.BlockSpec((BATCH, TM_EVEN, D_MODEL), prev),
        out_shape=jax.ShapeDtypeStruct((BATCH, SEQ, D_MODEL), F32),
        scratch_shapes=[
            pltpu.VMEM((BATCH, GLA_HEADS, GLA_HV, GLA_HK), F32),
            pltpu.VMEM((BATCH, POOL_HALO + TM_EVEN, POOL_WIDTH), F32),
            pltpu.VMEM((BATCH * TM_EVEN, D_MODEL), F32),
            cat,
            cat,
        ],
        compiler_params=_params("arbitrary"),
        name="even_mixer",
    )(x3, x3, proj.reshape(BATCH, SEQ, EVEN_PROJ), modl,
      wgin, wglr, bglr, normg, poolw, pscale, wout, lng, lnb)
    return out.reshape(TOKENS, D_MODEL)


def _odd_mixer_kernel(x_ref, proj_ref, mod_ref, convw_ref, convb_ref, wrg_ref, brg_ref,
                      wig_ref, big_ref, lam_ref, wout_ref, lng_ref, lnb_ref, o_ref,
                      xbuf_scr, hc_scr, z_scr):
    tm = TM_MIX
    i = pl.program_id(1)

    @pl.when(i == 0)
    def _():
        xbuf_scr[0:CONV_HALO, :] = jnp.zeros((CONV_HALO, LRU_WIDTH), F32)
        hc_scr[...] = jnp.zeros_like(hc_scr)

    xbuf_scr[CONV_HALO:CONV_HALO + tm, :] = proj_ref[:, LRU_WIDTH:2 * LRU_WIDTH]
    pos = lax.broadcasted_iota(jnp.int32, (tm, 1), 0) + i * tm

    sub8 = lax.broadcasted_iota(jnp.int32, (tm, LRU_HD), 0) & 7

    for hb in range(LRU_HEADS):
        c0 = hb * LRU_HD
        xz = xbuf_scr[0:CONV_HALO + tm, c0:c0 + LRU_HD]
        xc = convb_ref[:, c0:c0 + LRU_HD] + convw_ref[CONV_W - 1:CONV_W, c0:c0 + LRU_HD] * xz[CONV_HALO:, :]
        for j in range(1, CONV_W):
            kk = CONV_W - 1 - j
            xc = xc + convw_ref[kk:kk + 1, c0:c0 + LRU_HD] * pltpu.roll(xz, j, 0)[CONV_HALO:, :]
        xcb = xc.astype(BF16)
        rg = _sigmoid(_dot(xcb, wrg_ref[hb]) + brg_ref[:, c0:c0 + LRU_HD])
        ig = _sigmoid(_dot(xcb, wig_ref[hb]) + big_ref[:, c0:c0 + LRU_HD])
        log_a = (-LRU_C) * rg * _softplus(-lam_ref[:, c0:c0 + LRU_HD])
        a = jnp.exp(log_a)
        th = jnp.tanh(log_a)
        m2 = -2.0 * th / (1.0 - th)
        mult = jnp.where(m2 > 0.0, m2 * lax.rsqrt(m2), 0.0)
        mult = jnp.where(pos == 0, 1.0, mult)
        u = (xc * ig) * mult

        for d in (1, 2, 4):
            keep = sub8 >= d
            u = a * jnp.where(keep, pltpu.roll(u, d, 0), 0.0) + u
            a = a * jnp.where(keep, pltpu.roll(a, d, 0), 1.0)
        carry = hc_scr[:, c0:c0 + LRU_HD]
        blocks = []
        for blk in range(tm // 8):
            hb8 = a[blk * 8:(blk + 1) * 8, :] * carry + u[blk * 8:(blk + 1) * 8, :]
            carry = hb8[7:8, :]
            blocks.append(hb8)
        hseq = jnp.concatenate(blocks, axis=0)
        hc_scr[:, c0:c0 + LRU_HD] = carry
        gate = proj_ref[:, c0:c0 + LRU_HD]
        z_scr[:, c0:c0 + LRU_HD] = (_gelu_tanh(gate) * hseq).astype(BF16)

    xbuf_scr[0:CONV_HALO, :] = xbuf_scr[tm:tm + CONV_HALO, :]
    y = _dot(z_scr[...], wout_ref[...])
    m = mod_ref[0]
    o_ref[...] = _ln_residual(x_ref[...], y, m[2:3, :], lng_ref[...], lnb_ref[...])


def _odd_mixer(x2, proj, modl, convw, convb, wrg, brg, wig, big, lam, wout, lng, lnb):
    nt = SEQ // TM_MIX
    tok = lambda b, i: (b * nt + i, 0)
    const2 = lambda b, i: (0, 0)
    const3 = lambda b, i: (0, 0, 0)
    return pl.pallas_call(
        _odd_mixer_kernel,
        grid=(BATCH, nt),
        in_specs=[
            pl.BlockSpec((TM_MIX, D_MODEL), tok),
            pl.BlockSpec((TM_MIX, 2 * LRU_WIDTH), tok),
            pl.BlockSpec((1, 6, D_MODEL), lambda b, i: (b, 0, 0)),
            pl.BlockSpec((CONV_W, LRU_WIDTH), const2),
            pl.BlockSpec((1, LRU_WIDTH), const2),
            pl.BlockSpec((LRU_HEADS, LRU_HD, LRU_HD), const3),
            pl.BlockSpec((1, LRU_WIDTH), const2),
            pl.BlockSpec((LRU_HEADS, LRU_HD, LRU_HD), const3),
            pl.BlockSpec((1, LRU_WIDTH), const2),
            pl.BlockSpec((1, LRU_WIDTH), const2),
            pl.BlockSpec((LRU_WIDTH, D_MODEL), const2),
            pl.BlockSpec((1, D_MODEL), const2),
            pl.BlockSpec((1, D_MODEL), const2),
        ],
        out_specs=pl.BlockSpec((TM_MIX, D_MODEL), tok),
        out_shape=jax.ShapeDtypeStruct((TOKENS, D_MODEL), F32),
        scratch_shapes=[
            pltpu.VMEM((CONV_HALO + TM_MIX, LRU_WIDTH), F32),
            pltpu.VMEM((1, LRU_WIDTH), F32),
            pltpu.VMEM((TM_MIX, LRU_WIDTH), BF16),
        ],
        compiler_params=_params("arbitrary", "arbitrary"),
        name="odd_mixer",
    )(x2, proj, modl, convw, convb, wrg, brg, wig, big, lam, wout, lng, lnb)


def _first_max4(v):
    m = jnp.maximum(jnp.maximum(v[0], v[1]), jnp.maximum(v[2], v[3]))
    idx = jnp.where(v[0] == m, 0, jnp.where(v[1] == m, 1, jnp.where(v[2] == m, 2, 3)))
    return m, idx


def _route(h, wrt_ref, bias_ref):
    h_hi = h.astype(BF16)
    h_lo = (h - h_hi.astype(F32)).astype(BF16)
    w = wrt_ref[...]
    w_hi = w.astype(BF16)
    w_lo = (w - w_hi.astype(F32)).astype(BF16)
    dg = functools.partial(lax.dot_general, dimension_numbers=_NT, preferred_element_type=F32)
    logits = dg(w_hi, h_hi) + dg(w_hi, h_lo) + dg(w_lo, h_hi)
    score = _sigmoid(logits)
    sel = score + bias_ref[...]

    neg = -jnp.inf
    s_rows = [sel[e:e + 1, :] for e in range(N_EXPERTS)]
    group_score, first, second = [], [], []
    for g in range(N_GROUPS):
        v = s_rows[g * EXP_PER_GROUP:(g + 1) * EXP_PER_GROUP]
        m1, i1 = _first_max4(v)
        v2 = [jnp.where(i1 == j, neg, v[j]) for j in range(EXP_PER_GROUP)]
        m2, i2 = _first_max4(v2)
        group_score.append(m1 + m2)
        first.append(i1)
        second.append(i2)
    _, gbest = _first_max4(group_score)
    pick = lambda rows: jnp.where(gbest == 0, rows[0], jnp.where(gbest == 1, rows[1],
                                  jnp.where(gbest == 2, rows[2], rows[3])))
    e1 = gbest * EXP_PER_GROUP + pick(first)
    e2 = gbest * EXP_PER_GROUP + pick(second)
    s1 = jnp.zeros_like(s_rows[0])
    s2 = jnp.zeros_like(s_rows[0])
    for e in range(N_EXPERTS):
        row = score[e:e + 1, :]
        s1 = s1 + jnp.where(e1 == e, row, 0.0)
        s2 = s2 + jnp.where(e2 == e, row, 0.0)
    den = s1 + s2
    return e1, e2, s1 / den, s2 / den


def _router_kernel(x_ref, mod_ref, wrt_ref, bias_ref, hf_ref, eidx_ref, wts_ref):
    m = mod_ref[0]
    h = x_ref[...] * (1.0 + m[4:5, :]) + m[3:4, :]
    hf_ref[...] = h
    e1, e2, w1, w2 = _route(h, wrt_ref, bias_ref)
    eidx_ref[0:1, :] = e1
    eidx_ref[1:2, :] = e2
    wts_ref[0:1, :] = w1
    wts_ref[1:2, :] = w2


def _router(x2, modl, wrt, bias_col):
    nt = SEQ // TM_ROUTE
    tok = lambda b, i: (b * nt + i, 0)
    tokT = lambda b, i: (0, b * nt + i)
    return pl.pallas_call(
        _router_kernel,
        grid=(BATCH, nt),
        in_specs=[
            pl.BlockSpec((TM_ROUTE, D_MODEL), tok),
            pl.BlockSpec((1, 6, D_MODEL), lambda b, i: (b, 0, 0)),
            pl.BlockSpec((N_EXPERTS, D_MODEL), lambda b, i: (0, 0)),
            pl.BlockSpec((N_EXPERTS, 1), lambda b, i: (0, 0)),
        ],
        out_specs=[
            pl.BlockSpec((TM_ROUTE, D_MODEL), tok),
            pl.BlockSpec((2, TM_ROUTE), tokT),
            pl.BlockSpec((2, TM_ROUTE), tokT),
        ],
        out_shape=[
            jax.ShapeDtypeStruct((TOKENS, D_MODEL), F32),
            jax.ShapeDtypeStruct((2, TOKENS), jnp.int32),
            jax.ShapeDtypeStruct((2, TOKENS), F32),
        ],
        compiler_params=_params("arbitrary", "arbitrary"),
        name="moe_router",
    )(x2, modl, wrt, bias_col)


def _prefix_experts(v):
    eidx = lax.broadcasted_iota(jnp.int32, v.shape, 0)
    d = 1
    while d < N_EXPERTS:
        v = v + jnp.where(eidx >= d, pltpu.roll(v, d, 0), 0.0)
        d *= 2
    return v


def _plan_kernel(e_ref, pos_ref, meta_ref, cnt_scr, run_scr, offs_scr):
    tq = TM_ROUTE
    p = pl.program_id(0)
    i = pl.program_id(1)
    e = e_ref[...]
    eid = lax.broadcasted_iota(jnp.int32, (N_EXPERTS, tq), 0)
    oh0 = jnp.where(eid == e[0:1, :], 1.0, 0.0)
    oh1 = jnp.where(eid == e[1:2, :], 1.0, 0.0)

    @pl.when(p == 0)
    def _():
        @pl.when(i == 0)
        def _():
            cnt_scr[...] = jnp.zeros_like(cnt_scr)

        cnt_scr[...] = cnt_scr[...] + jnp.sum(oh0 + oh1, axis=1, keepdims=True)

    @pl.when((p == 1) & (i == 0))
    def _():
        cnt = cnt_scr[...]
        offs = _prefix_experts(cnt) - cnt
        offs_scr[...] = offs
        run_scr[...] = jnp.zeros_like(run_scr)
        nseg = jnp.zeros_like(cnt)
        for k in range(MAX_SEG_PER_EXPERT):
            nseg = nseg + jnp.where(cnt > float(k * SEG_CAP), 1.0, 0.0)
        seg_end = _prefix_experts(nseg)
        seg_begin = seg_end - nseg
        n_seg = jnp.max(seg_end, axis=0, keepdims=True)
        s = lax.broadcasted_iota(jnp.int32, cnt.shape, 1).astype(F32)
        expert = jnp.sum(jnp.where(seg_end <= jnp.minimum(s, n_seg - 1.0), 1.0, 0.0), axis=0, keepdims=True)
        member = jnp.where(seg_begin <= s, jnp.where(s < seg_end, 1.0, 0.0), 0.0)
        done = (s - seg_begin) * float(SEG_CAP)
        rows = jnp.sum(member * jnp.minimum(float(SEG_CAP), cnt - done), axis=0, keepdims=True)
        start = jnp.sum(member * (offs + done), axis=0, keepdims=True)
        meta_ref[...] = jnp.zeros_like(meta_ref)
        meta_ref[0:1, :] = expert.astype(jnp.int32)
        meta_ref[1:2, :] = start.astype(jnp.int32)
        meta_ref[2:3, :] = rows.astype(jnp.int32)
        meta_ref[3:4, :] = n_seg.astype(jnp.int32)

    @pl.when(p == 1)
    def _():
        rj = lax.broadcasted_iota(jnp.int32, (tq, tq), 0)
        ct = lax.broadcasted_iota(jnp.int32, (tq, tq), 1)
        upper = jnp.where(rj <= ct, 1.0, 0.0).astype(BF16)
        pre0 = _dot(oh0.astype(BF16), upper)
        pre1 = _dot(oh1.astype(BF16), upper)
        tot0 = pre0[:, tq - 1:tq]
        tot1 = pre1[:, tq - 1:tq]
        base = offs_scr[:, 0:1] + run_scr[:, 0:1]
        r0 = jnp.sum(oh0 * (base + pre0 - 1.0), axis=0, keepdims=True)
        r1 = jnp.sum(oh1 * (base + tot0 + pre1 - 1.0), axis=0, keepdims=True)
        pos_ref[0:1, :] = r0.astype(jnp.int32)
        pos_ref[1:2, :] = r1.astype(jnp.int32)
        run_scr[...] = run_scr[...] + (tot0 + tot1)


def _plan(eidx):
    nt = TOKENS // TM_ROUTE
    acc = pltpu.VMEM((N_EXPERTS, LANES), F32)
    return pl.pallas_call(
        _plan_kernel,
        grid=(2, nt),
        in_specs=[pl.BlockSpec((2, TM_ROUTE), lambda p, i: (0, i))],
        out_specs=[
            pl.BlockSpec((2, TM_ROUTE), lambda p, i: (0, i * p)),
            pl.BlockSpec((8, LANES), lambda p, i: (0, 0)),
        ],
        out_shape=[
            jax.ShapeDtypeStruct((2, TOKENS), jnp.int32),
            jax.ShapeDtypeStruct((8, LANES), jnp.int32),
        ],
        scratch_shapes=[acc, acc, acc],
        compiler_params=_params("arbitrary", "arbitrary"),
        name="moe_plan",
    )(eidx)


def _invert_kernel(pos_ref, inv_ref):
    base = pl.program_id(0) * INV_CHUNK

    @pl.when(pl.program_id(0) == 0)
    def _():
        for k in range(INV_SPARE):
            inv_ref[2 * TOKENS + k] = 0

    def body(k, carry):
        inv_ref[pos_ref[k]] = base + k
        return carry

    lax.fori_loop(0, INV_CHUNK, body, 0, unroll=8)


def _invert(pos_flat):
    return pl.pallas_call(
        _invert_kernel,
        grid=(2 * TOKENS // INV_CHUNK,),
        in_specs=[pl.BlockSpec((INV_CHUNK,), lambda i: (i,), memory_space=pltpu.SMEM)],
        out_specs=pl.BlockSpec(memory_space=pltpu.SMEM),
        out_shape=jax.ShapeDtypeStruct((2 * TOKENS + INV_SPARE,), jnp.int32),
        compiler_params=_params("arbitrary"),
        name="moe_invert",
    )(pos_flat)


def _row(ref3, row):
    return ref3.at[row >> 3, pl.ds(row & 7, 1), :]


def _wait_rows(n, max_rows, src3, dst3, sem):
    p = 1
    while p <= max_rows:
        @pl.when((n & p) != 0)
        def _(p=p):
            if p < 8:
                cp = pltpu.make_async_copy(src3.at[0, pl.ds(0, p), :], dst3.at[0, pl.ds(0, p), :], sem)
            else:
                cp = pltpu.make_async_copy(src3.at[pl.ds(0, p // 8)], dst3.at[pl.ds(0, p // 8)], sem)
            cp.wait()

        p *= 2


def _ffn_kernel(se_ref, ss_ref, sr_ref, ns_ref, inv_ref, hf_hbm, wg_hbm, wu_hbm, wd_hbm, out_hbm,
                stag, wbf, xb0, xb1, act_scr, ybuf, sem_w, sem_x, sem_y, *, layer):
    s = pl.program_id(0)
    ph = pl.program_id(1)
    n_seg = ns_ref[0]
    e = se_ref[s]
    start = ss_ref[s]
    rows = sr_ref[s]
    n_sub = (rows + (SEG_SUB - 1)) >> SEG_SUB_LOG2
    half = D_MODEL // 2
    hh = D_EXPERT // 2
    nxt = jnp.minimum(s + 1, N_SEG - 1)
    next_start = ss_ref[nxt]
    next_chunks = jnp.where(s + 1 < n_seg, ((sr_ref[nxt] + (SEG_SUB - 1)) >> SEG_SUB_LOG2) * CHUNKS_PER_SUB, 0)

    def weight_copies(phase, expert, slot):
        if phase < 2:
            cols = pl.ds(phase * hh, hh)
            return [pltpu.make_async_copy(w_hbm.at[layer, expert, :, cols],
                                          stag.at[slot, :, pl.ds(c * hh, hh)], sem_w.at[slot])
                    for c, w_hbm in enumerate((wg_hbm, wu_hbm))]
        return [pltpu.make_async_copy(wd_hbm.at[layer, expert, :, pl.ds(c * half, half)],
                                      stag.at[slot, pl.ds(c * D_EXPERT, D_EXPERT), :], sem_w.at[slot])
                for c in range(2)]

    def gather_chunk(seg_start, c, xdst, sem):
        base = seg_start + c * GATHER_CHUNK
        g0 = c * (GATHER_CHUNK // 8)
        for i in range(GATHER_CHUNK // 8):
            for j in range(8):
                tok = inv_ref[base + (i * 8 + j)] & (TOKENS - 1)
                pltpu.make_async_copy(hf_hbm.at[pl.ds(tok, 1), :], xdst.at[g0 + i, pl.ds(j, 1), :], sem).start()

    blk = D_MODEL // W_SLICES

    def wait_weights(phase, slot):
        for cp in weight_copies(phase, e, slot):
            cp.wait()

    def cast_slice(kk, slot):
        w = stag[slot, kk * blk:(kk + 1) * blk, :].astype(BF16)
        wbf[kk * blk:(kk + 1) * blk, :] = w
        return w

    def gate_up_half(phase, slot, xcur, xnext, sem_next):
        first_chunk = phase * n_sub
        with_gather = jnp.clip(next_chunks - first_chunk, 0, n_sub)

        def finish(k, gu, gather):
            r0 = pl.multiple_of(k * SEG_SUB, SEG_SUB)
            g = gu[:, 0:hh]
            act_scr[pl.ds(r0, SEG_SUB), phase * hh:(phase + 1) * hh] = (
                (g * _sigmoid(g)) * gu[:, hh:2 * hh]).astype(BF16)
            if gather:
                gather_chunk(next_start, first_chunk + k, xnext, sem_next)

        def block(k, gather):
            x = xcur[pl.ds(k * (SEG_SUB // 8), SEG_SUB // 8)].reshape(SEG_SUB, D_MODEL)
            finish(k, _dot(x.astype(BF16), wbf[...]), gather)

        def first_block(gather):
            x = xcur[pl.ds(0, SEG_SUB // 8)].reshape(SEG_SUB, D_MODEL)
            gu = None
            for kk in range(W_SLICES):
                part = _dot(x[:, kk * blk:(kk + 1) * blk].astype(BF16), cast_slice(kk, slot))
                gu = part if gu is None else gu + part
            finish(0, gu, gather)

        def gathering(k, carry):
            block(k, True)
            return carry

        def plain(k, carry):
            block(k, False)
            return carry

        @pl.when(with_gather > 0)
        def _():
            first_block(True)

        @pl.when(with_gather == 0)
        def _():
            first_block(False)

        lax.fori_loop(1, jnp.maximum(with_gather, 1), gathering, 0)
        lax.fori_loop(jnp.maximum(with_gather, 1), n_sub, plain, 0)

    def leftover_chunks(seg_start, first, total, xdst, sem):
        def body(c, carry):
            gather_chunk(seg_start, c, xdst, sem)
            return carry

        lax.fori_loop(first, total, body, 0)

    def wait_gathered(xcur, sem):
        _wait_rows(n_sub * SEG_SUB, SEG_CAP, xcur, xcur, sem)

    valid = s < n_seg
    q = 3 * s + ph
    slot = q % 2

    def phase0(xcur, xnext, sem_cur, sem_next):
        @pl.when(s == 0)
        def _():
            xb0[...] = jnp.zeros_like(xb0)
            xb1[...] = jnp.zeros_like(xb1)
            for cp in weight_copies(0, e, slot):
                cp.start()
            leftover_chunks(start, 0, n_sub * CHUNKS_PER_SUB, xcur, sem_cur)

        for cp in weight_copies(1, e, 1 - slot):
            cp.start()
        wait_gathered(xcur, sem_cur)
        wait_weights(0, slot)
        gate_up_half(0, slot, xcur, xnext, sem_next)

    def phase1(xcur, xnext, sem_next):
        for cp in weight_copies(2, e, 1 - slot):
            cp.start()
        wait_weights(1, slot)
        gate_up_half(1, slot, xcur, xnext, sem_next)
        leftover_chunks(next_start, 2 * n_sub, next_chunks, xnext, sem_next)

    even = (s % 2) == 0

    @pl.when(valid & (ph == 0) & even)
    def _():
        phase0(xb0, xb1, sem_x.at[0], sem_x.at[1])

    @pl.when(valid & (ph == 0) & jnp.logical_not(even))
    def _():
        phase0(xb1, xb0, sem_x.at[1], sem_x.at[0])

    @pl.when(valid & (ph == 1) & even)
    def _():
        phase1(xb0, xb1, sem_x.at[1])

    @pl.when(valid & (ph == 1) & jnp.logical_not(even))
    def _():
        phase1(xb1, xb0, sem_x.at[0])

    @pl.when(valid & (ph == 2))
    def _():
        @pl.when(s + 1 < n_seg)
        def _():
            for cp in weight_copies(0, se_ref[s + 1], 1 - slot):
                cp.start()

        wait_weights(2, slot)

        def out_row(pair):
            return out_hbm.at[pl.ds(pair, 1), :]

        def down(k):
            r0 = pl.multiple_of(k * SEG_SUB, SEG_SUB)
            act = act_scr[pl.ds(r0, SEG_SUB), :]
            for c in range(2):
                ybuf[k % 2, :, :, c * half:(c + 1) * half] = _dot(
                    act, wbf[c * D_EXPERT:(c + 1) * D_EXPERT, :]).reshape(SEG_SUB // 8, 8, half)

        def scatter_full(k):
            base = start + k * SEG_SUB
            for i in range(SEG_SUB // 8):
                for j in range(8):
                    pltpu.make_async_copy(ybuf.at[k % 2, i, pl.ds(j, 1), :], out_row(inv_ref[base + (i * 8 + j)]),
                                          sem_y.at[k % 2]).start(priority=1)

        def wait_scattered(n, k):
            _wait_rows(n, SEG_SUB, ybuf.at[k % 2], ybuf.at[k % 2], sem_y.at[k % 2])

        def down_first():
            act = act_scr[0:SEG_SUB, :]
            per_half = W_SLICES // 2
            for c in range(2):
                acc = None
                for kk in range(per_half):
                    part = _dot(act[:, kk * blk:(kk + 1) * blk], cast_slice(c * per_half + kk, slot))
                    acc = part if acc is None else acc + part
                ybuf[0, :, :, c * half:(c + 1) * half] = acc.reshape(SEG_SUB // 8, 8, half)

        down_first()

        @pl.when(n_sub >= 2)
        def _():
            scatter_full(0)
            down(1)

        def steady(k, carry):
            wait_scattered(SEG_SUB, k)
            scatter_full(k - 1)
            down(k)
            return carry

        lax.fori_loop(2, n_sub, steady, 0)

        last = n_sub - 1
        n_rows = rows - last * SEG_SUB
        base = start + last * SEG_SUB

        def group(i, c):
            for j in range(8):
                pltpu.make_async_copy(ybuf.at[last % 2, i, pl.ds(j, 1), :], out_row(inv_ref[base + i * 8 + j]),
                                      sem_y.at[last % 2]).start(priority=1)
            return c

        lax.fori_loop(0, n_rows >> 3, group, 0)

        def single(r, c):
            pltpu.make_async_copy(_row(ybuf.at[last % 2], r), out_row(inv_ref[base + r]),
                                  sem_y.at[last % 2]).start(priority=1)
            return c

        lax.fori_loop((n_rows >> 3) << 3, n_rows, single, 0)

        @pl.when(n_sub >= 2)
        def _():
            wait_scattered(SEG_SUB, n_sub)
        wait_scattered(n_rows, last)


def _ffn(layer, seg_expert, seg_start, seg_rows, n_seg, inv, hf, wg, wu, wd):
    assert D_EXPERT * 2 == D_MODEL
    anyspec = pl.BlockSpec(memory_space=pl.ANY)
    grid_spec = pltpu.PrefetchScalarGridSpec(
        num_scalar_prefetch=5,
        grid=(N_SEG, 3),
        in_specs=[anyspec, anyspec, anyspec, anyspec],
        out_specs=anyspec,
        scratch_shapes=[
            pltpu.VMEM((2, D_MODEL, D_EXPERT), F32),
            pltpu.VMEM((D_MODEL, D_EXPERT), BF16),
            pltpu.VMEM((SEG_CAP // 8, 8, D_MODEL), F32),
            pltpu.VMEM((SEG_CAP // 8, 8, D_MODEL), F32),
            pltpu.VMEM((SEG_CAP, D_EXPERT), BF16),
            pltpu.VMEM((2, SEG_SUB // 8, 8, D_MODEL), F32),
            pltpu.SemaphoreType.DMA((2,)),
            pltpu.SemaphoreType.DMA((2,)),
            pltpu.SemaphoreType.DMA((2,)),
        ],
    )
    return pl.pallas_call(
        functools.partial(_ffn_kernel, layer=layer),
        grid_spec=grid_spec,
        out_shape=jax.ShapeDtypeStruct((2 * TOKENS, D_MODEL), F32),
        compiler_params=_params("arbitrary", "arbitrary"),
        name="moe_ffn",
    )(seg_expert, seg_start, seg_rows, n_seg, inv, hf, wg, wu, wd)


def _combine_kernel(x_ref, mod_ref, w_ref, lng_ref, lnb_ref, y0_ref, y1_ref, o_ref):
    w = w_ref[...]
    y = w[:, 0:1] * y0_ref[...] + w[:, 1:2] * y1_ref[...]
    m = mod_ref[0]
    o_ref[...] = _ln_residual(x_ref[...], y, m[5:6, :], lng_ref[...], lnb_ref[...])


def _combine(x2, modl, wts_col, lng, lnb, y2):
    nt = SEQ // TM_COMB
    tok = lambda b, i: (b * nt + i, 0)
    tok1 = lambda b, i: (TOKENS // TM_COMB + b * nt + i, 0)
    return pl.pallas_call(
        _combine_kernel,
        grid=(BATCH, nt),
        in_specs=[
            pl.BlockSpec((TM_COMB, D_MODEL), tok),
            pl.BlockSpec((1, 6, D_MODEL), lambda b, i: (b, 0, 0)),
            pl.BlockSpec((TM_COMB, 2), tok),
            pl.BlockSpec((1, D_MODEL), lambda b, i: (0, 0)),
            pl.BlockSpec((1, D_MODEL), lambda b, i: (0, 0)),
            pl.BlockSpec((TM_COMB, D_MODEL), tok),
            pl.BlockSpec((TM_COMB, D_MODEL), tok1),
        ],
        out_specs=pl.BlockSpec((TM_COMB, D_MODEL), tok),
        out_shape=jax.ShapeDtypeStruct((TOKENS, D_MODEL), F32),
        compiler_params=_params("arbitrary", "arbitrary"),
        name="moe_combine",
    )(x2, modl, wts_col, lng, lnb, y2, y2)


def _moe(layer, x2, modl, wrt, bias_col, wg, wu, wd, lng, lnb):
    hf, eidx, wts = _router(x2, modl, wrt, bias_col)
    pos, meta = _plan(eidx)
    inv = _invert(pos.reshape(-1))
    y2 = _ffn(layer, meta[0, :N_SEG], meta[1, :N_SEG], meta[2, :N_SEG], meta[3, :1], inv, hf, wg, wu, wd)
    return _combine(x2, modl, wts.T, lng, lnb, y2)


def kernel(x, c, w_ada, b_ada, ln_mix_g, ln_mix_b, ln_ffn_g, ln_ffn_b, even_w_in, gla_w_glr, gla_b_glr, gla_norm_g, pool_w, pool_scale, even_w_out, odd_w_in, lru_conv_w, lru_conv_b, lru_w_rg, lru_b_rg, lru_w_ig, lru_b_ig, lru_lambda, odd_w_out, w_router, router_bias, exp_w_gate, exp_w_up, exp_w_down):
    mod = _adaln_mod(c, w_ada, b_ada)
    x2 = x.reshape(TOKENS, D_MODEL)
    wrt = w_router.T
    bias_col = router_bias.reshape(N_EXPERTS, 1)
    row = lambda a: a.reshape(1, -1)
    for l in range(DEPTH):
        j = l // 2
        modl = mod[l]
        if l % 2 == 0:
            w_in = even_w_in[j]
            w_main = jnp.concatenate([w_in[:, :SPLIT_R], w_in[:, SPLIT_G:]], axis=1).astype(BF16)
            wgin = jnp.pad(w_in[:, SPLIT_R:SPLIT_G], ((0, 0), (0, LANES - GLA_RANK))).astype(BF16)
            wglr = jnp.pad(gla_w_glr[j], ((0, LANES - GLA_RANK), (0, 0))).astype(BF16)
            proj = _inproj(x2, modl, w_main)
            x2 = _even_mixer(x2, proj, modl, wgin, wglr, row(gla_b_glr[j]), row(gla_norm_g[j]),
                             pool_w[j].astype(BF16), row(pool_scale[j]), even_w_out[j].astype(BF16),
                             row(ln_mix_g[l]), row(ln_mix_b[l]))
        else:
            proj = _inproj(x2, modl, odd_w_in[j].astype(BF16))
            x2 = _odd_mixer(x2, proj, modl, lru_conv_w[j], row(lru_conv_b[j]),
                            lru_w_rg[j].astype(BF16), row(lru_b_rg[j]), lru_w_ig[j].astype(BF16),
                            row(lru_b_ig[j]), row(lru_lambda[j]), odd_w_out[j].astype(BF16),
                            row(ln_mix_g[l]), row(ln_mix_b[l]))
        x2 = _moe(l, x2, modl, wrt, bias_col, exp_w_gate, exp_w_up, exp_w_down,
                  row(ln_ffn_g[l]), row(ln_ffn_b[l]))
    return x2.reshape(BATCH, SEQ, D_MODEL)
```

```python
import functools

import jax
import jax.numpy as jnp
from jax import lax
from jax.experimental import pallas as pl
from jax.experimental.pallas import tpu as pltpu

F32 = jnp.float32
BF16 = jnp.bfloat16

D_MODEL = 2048
BATCH = 2
SEQ = 4096
TOKENS = BATCH * SEQ
DEPTH = 4
ALPHA = (2 * DEPTH) ** 0.25
LN_EPS = 1e-5

GLA_HEADS = 4
GLA_DK = D_MODEL // 4
GLA_DV = D_MODEL // 2
GLA_HK = GLA_DK // GLA_HEADS
GLA_HV = GLA_DV // GLA_HEADS
GLA_RANK = 16
GLA_TAU = 16.0
GLA_CHUNK = 64
POOL_WIDTH = D_MODEL // 2
POOL_WINDOWS = (2, 4, 8, 16)
POOL_GC = POOL_WIDTH // len(POOL_WINDOWS)
POOL_HALO = 16
SPLIT_R = 2 * GLA_DK + 2 * GLA_DV
SPLIT_G = SPLIT_R + GLA_RANK
EVEN_PROJ = SPLIT_R + POOL_WIDTH

LRU_WIDTH = 5 * D_MODEL // 4
LRU_HEADS = 10
LRU_HD = LRU_WIDTH // LRU_HEADS
CONV_W = 4
CONV_HALO = 8
LRU_C = 8.0

N_EXPERTS = 16
N_GROUPS = 4
EXP_PER_GROUP = N_EXPERTS // N_GROUPS
D_EXPERT = D_MODEL // 2

LANES = 128
VMEM_LIMIT = 56 * 1024 * 1024

TM_PROJ = 1024
TN_PROJ = 1024
TM_MIX = 256
TM_EVEN = 128
OUT_SLAB = 256
OUT_SLAB_EVERY = (BATCH * (TM_EVEN // GLA_CHUNK) * GLA_HEADS) // (D_MODEL // OUT_SLAB)
TM_ROUTE = 512
TM_COMB = 256
SEG_SUB_LOG2 = 8
SEG_SUB = 1 << SEG_SUB_LOG2
GATHER_CHUNK = 128
CHUNKS_PER_SUB = SEG_SUB // GATHER_CHUNK
SEG_CAP = 5 * SEG_SUB
MAX_SEG_PER_EXPERT = -(-TOKENS // SEG_CAP)
N_SEG = N_EXPERTS + (2 * TOKENS) // SEG_CAP + 1
W_SLICES = 8
INV_CHUNK = 2048
INV_SPARE = SEG_SUB

_NT = (((1,), (1,)), ((), ()))


def _sigmoid(x):
    return 1.0 / (1.0 + jnp.exp(-x))


def _softplus(x):
    return jnp.maximum(x, 0.0) + jnp.log1p(jnp.exp(-jnp.abs(x)))


def _gelu_tanh(x):
    return 0.5 * x * (1.0 + jnp.tanh(0.7978845608028654 * (x + 0.044715 * (x * x * x))))


def _dot(a, b):
    return jnp.dot(a, b, preferred_element_type=F32)


def _ln_residual(x, y, gate, ln_g, ln_b):
    r = ALPHA * x + (1.0 + gate) * y
    mu = jnp.mean(r, axis=-1, keepdims=True)
    d = r - mu
    var = jnp.mean(d * d, axis=-1, keepdims=True)
    return d * lax.rsqrt(var + LN_EPS) * ln_g + ln_b


def _params(*sem):
    return pltpu.CompilerParams(dimension_semantics=sem, vmem_limit_bytes=VMEM_LIMIT)


def _mod_kernel(c_ref, w_ref, b_ref, o_ref):
    c = c_ref[...]
    sc = (c * _sigmoid(c)).astype(BF16)
    o_ref[0] = _dot(sc, w_ref[0].astype(BF16)) + b_ref[0]


def _adaln_mod(c, w_ada, b_ada):
    tn = 1024
    n = 6 * D_MODEL
    c8 = jnp.pad(c, ((0, 8 - BATCH), (0, 0)))
    out = pl.pallas_call(
        _mod_kernel,
        grid=(DEPTH, n // tn),
        in_specs=[
            pl.BlockSpec((8, D_MODEL), lambda l, j: (0, 0)),
            pl.BlockSpec((1, D_MODEL, tn), lambda l, j: (l, 0, j)),
            pl.BlockSpec((1, 1, tn), lambda l, j: (l, 0, j)),
        ],
        out_specs=pl.BlockSpec((1, 8, tn), lambda l, j: (l, 0, j)),
        out_shape=jax.ShapeDtypeStruct((DEPTH, 8, n), F32),
        compiler_params=_params("arbitrary", "arbitrary"),
        name="adaln_mod",
    )(c8, w_ada, b_ada.reshape(DEPTH, 1, n))
    return out[:, :BATCH, :].reshape(DEPTH, BATCH, 6, D_MODEL)


def _inproj_kernel(x_ref, mod_ref, w_ref, o_ref, h_scr):
    @pl.when(pl.program_id(2) == 0)
    def _():
        m = mod_ref[0]
        h_scr[...] = (x_ref[...] * (1.0 + m[1:2, :]) + m[0:1, :]).astype(BF16)

    o_ref[...] = _dot(h_scr[...], w_ref[...])


def _inproj(x2, modl, w_bf16):
    n = w_bf16.shape[1]
    ntm = SEQ // TM_PROJ
    return pl.pallas_call(
        _inproj_kernel,
        grid=(BATCH, ntm, n // TN_PROJ),
        in_specs=[
            pl.BlockSpec((TM_PROJ, D_MODEL), lambda b, i, j: (b * ntm + i, 0)),
            pl.BlockSpec((1, 6, D_MODEL), lambda b, i, j: (b, 0, 0)),
            pl.BlockSpec((D_MODEL, TN_PROJ), lambda b, i, j: (0, j)),
        ],
        out_specs=pl.BlockSpec((TM_PROJ, TN_PROJ), lambda b, i, j: (b * ntm + i, j)),
        out_shape=jax.ShapeDtypeStruct((TOKENS, n), F32),
        scratch_shapes=[pltpu.VMEM((TM_PROJ, D_MODEL), BF16)],
        compiler_params=_params("arbitrary", "arbitrary", "arbitrary"),
        name="inproj",
    )(x2, modl, w_bf16)


def _even_mixer_kernel(x_ref, xprev_ref, proj_ref, mod_ref, wgin_ref, wglr_ref, bglr_ref, normg_ref,
                       poolw_ref, pscale_ref, wout_ref, lng_ref, lnb_ref, o_ref,
                       st_scr, ubuf_scr, y_scr, cat0_scr, cat1_scr):
    i = pl.program_id(0)

    @pl.when(i == 0)
    def _():
        st_scr[...] = jnp.zeros_like(st_scr)
        ubuf_scr[:, 0:POOL_HALO, :] = jnp.zeros((BATCH, POOL_HALO, POOL_WIDTH), F32)
        cat1_scr[...] = jnp.zeros_like(cat1_scr)

    refs = (x_ref, xprev_ref, proj_ref, mod_ref, wgin_ref, wglr_ref, bglr_ref, normg_ref,
            poolw_ref, pscale_ref, wout_ref, lng_ref, lnb_ref, o_ref, st_scr, ubuf_scr, y_scr)

    @pl.when(i % 2 == 0)
    def _():
        _even_mixer_step(i, *refs, cat_scr=cat0_scr, cat_prev=cat1_scr)

    @pl.when(i % 2 == 1)
    def _():
        _even_mixer_step(i, *refs, cat_scr=cat1_scr, cat_prev=cat0_scr)


def _even_mixer_step(i, x_ref, xprev_ref, proj_ref, mod_ref, wgin_ref, wglr_ref, bglr_ref, normg_ref,
                     poolw_ref, pscale_ref, wout_ref, lng_ref, lnb_ref, o_ref, st_scr, ubuf_scr, y_scr,
                     *, cat_scr, cat_prev):
    tb = TM_EVEN
    tm = BATCH * tb
    ms = [mod_ref[b] for b in range(BATCH)]

    h = jnp.concatenate([(x_ref[b] * (1.0 + ms[b][1:2, :]) + ms[b][0:1, :]).astype(BF16)
                         for b in range(BATCH)], axis=0)

    glr = _dot(h, wgin_ref[...])
    z = _dot(glr.astype(BF16), wglr_ref[...]) + bglr_ref[...]
    log_a = -_softplus(-z) * (1.0 / GLA_TAU)

    la_hi = log_a.astype(BF16)
    res = log_a - la_hi.astype(F32)
    la_mid = res.astype(BF16)
    la_lo = (res - la_mid.astype(F32)).astype(BF16)
    ri = lax.broadcasted_iota(jnp.int32, (tm, tm), 0)
    ci = lax.broadcasted_iota(jnp.int32, (tm, tm), 1)
    back = ri - ci
    tril = jnp.where(back >= 0, jnp.where(back <= (ri & (GLA_CHUNK - 1)), 1.0, 0.0), 0.0).astype(BF16)
    bcum = _dot(tril, la_hi) + _dot(tril, la_mid) + _dot(tril, la_lo)

    r64 = lax.broadcasted_iota(jnp.int32, (GLA_CHUNK, GLA_CHUNK), 0)
    c64 = lax.broadcasted_iota(jnp.int32, (GLA_CHUNK, GLA_CHUNK), 1)
    causal = c64 <= r64
    tn_dims = (((0,), (0,)), ((), ()))

    for bi in range(BATCH):
        for c in range(tb // GLA_CHUNK):
            r0 = c * GLA_CHUNK
            j0 = bi * tb + r0
            for hh in range(GLA_HEADS):
                k0 = hh * GLA_HK
                v0 = hh * GLA_HV
                b = bcum[j0:j0 + GLA_CHUNK, k0:k0 + GLA_HK]
                b_last = b[GLA_CHUNK - 1:GLA_CHUNK, :]
                q = proj_ref[bi, r0:r0 + GLA_CHUNK, k0:k0 + GLA_HK]
                k = proj_ref[bi, r0:r0 + GLA_CHUNK, GLA_DK + k0:GLA_DK + k0 + GLA_HK]
                v = proj_ref[bi, r0:r0 + GLA_CHUNK, 2 * GLA_DK + v0:2 * GLA_DK + v0 + GLA_HV].astype(BF16)
                q_in = ((q * (GLA_HK ** -0.5)) * jnp.exp(b)).astype(BF16)
                k_in = (k * jnp.exp(-b)).astype(BF16)
                k_out = (k * jnp.exp(b_last - b)).astype(BF16)
                scores = lax.dot_general(q_in, k_in, _NT, preferred_element_type=F32)
                scores = jnp.where(causal, scores, 0.0).astype(BF16)
                st = st_scr[bi, hh]
                o = _dot(scores, v) + lax.dot_general(q_in, st.astype(BF16), _NT, preferred_element_type=F32)
                kv_t = lax.dot_general(v, k_out, tn_dims, preferred_element_type=F32)
                st_scr[bi, hh] = st * jnp.exp(b_last) + kv_t
                msq = jnp.mean(o * o, axis=-1, keepdims=True)
                on = o * lax.rsqrt(msq + LN_EPS) * normg_ref[:, v0:v0 + GLA_HV]
                r = proj_ref[bi, r0:r0 + GLA_CHUNK, SPLIT_R - GLA_DV + v0:SPLIT_R - GLA_DV + v0 + GLA_HV]
                cat_scr[j0:j0 + GLA_CHUNK, v0:v0 + GLA_HV] = (on * (r * _sigmoid(r))).astype(BF16)
                idx = (bi * (tb // GLA_CHUNK) + c) * GLA_HEADS + hh
                if idx % OUT_SLAB_EVERY == OUT_SLAB_EVERY - 1:
                    s0 = (idx // OUT_SLAB_EVERY) * OUT_SLAB
                    y_scr[:, s0:s0 + OUT_SLAB] = _dot(cat_prev[...], wout_ref[:, s0:s0 + OUT_SLAB])

    pos = lax.broadcasted_iota(jnp.int32, (tb, 1), 0) + jnp.minimum(i, SEQ // tb - 1) * tb
    for bi in range(BATCH):
        ubuf_scr[bi, POOL_HALO:POOL_HALO + tb, :] = proj_ref[bi, :, SPLIT_R:SPLIT_R + POOL_WIDTH]
    for g, w in enumerate(POOL_WINDOWS):
        c0 = g * POOL_GC
        cnt = jnp.minimum(pos + 1, w).astype(F32)
        ps = []
        for bi in range(BATCH):
            cur = ubuf_scr[bi, POOL_HALO:POOL_HALO + tb, c0:c0 + POOL_GC]
            acc = cur
            for kk in range(1, w):
                acc = acc + ubuf_scr[bi, POOL_HALO - kk:POOL_HALO - kk + tb, c0:c0 + POOL_GC]
            ps.append((acc / cnt - cur).astype(BF16))
        pp = _dot(jnp.concatenate(ps, axis=0), poolw_ref[g]) * pscale_ref[:, c0:c0 + POOL_GC]
        cat_scr[:, GLA_DV + c0:GLA_DV + c0 + POOL_GC] = pp.astype(BF16)
    for bi in range(BATCH):
        ubuf_scr[bi, 0:POOL_HALO, :] = ubuf_scr[bi, tb:tb + POOL_HALO, :]

    for bi in range(BATCH):
        o_ref[bi] = _ln_residual(xprev_ref[bi], y_scr[bi * tb:(bi + 1) * tb, :], ms[bi][2:3, :],
                                 lng_ref[...], lnb_ref[...])


def _even_mixer(x2, proj, modl, wgin, wglr, bglr, normg, poolw, pscale, wout, lng, lnb):
    nt = SEQ // TM_EVEN
    cur = lambda i: (0, jnp.minimum(i, nt - 1), 0)
    prev = lambda i: (0, jnp.maximum(i - 1, 0), 0)
    const2 = lambda i: (0, 0)
    const3 = lambda i: (0, 0, 0)
    x3 = x2.reshape(BATCH, SEQ, D_MODEL)
    cat = pltpu.VMEM((BATCH * TM_EVEN, D_MODEL), BF16)
    out = pl.pallas_call(
        _even_mixer_kernel,
        grid=(nt + 1,),
        in_specs=[
            pl.BlockSpec((BATCH, TM_EVEN, D_MODEL), cur),
            pl.BlockSpec((BATCH, TM_EVEN, D_MODEL), prev),
            pl.BlockSpec((BATCH, TM_EVEN, EVEN_PROJ), cur),
            pl.BlockSpec((BATCH, 6, D_MODEL), const3),
            pl.BlockSpec((D_MODEL, LANES), const2),
            pl.BlockSpec((LANES, GLA_DK), const2),
            pl.BlockSpec((1, GLA_DK), const2),
            pl.BlockSpec((1, GLA_DV), const2),
            pl.BlockSpec((len(POOL_WINDOWS), POOL_GC, POOL_GC), const3),
            pl.BlockSpec((1, POOL_WIDTH), const2),
            pl.BlockSpec((D_MODEL, D_MODEL), const2),
            pl.BlockSpec((1, D_MODEL), const2),
            pl.BlockSpec((1, D_MODEL), const2),
        ],
        out_specs=pl.BlockSpec((BATCH, TM_EVEN, D_MODEL), prev),
        out_shape=jax.ShapeDtypeStruct((BATCH, SEQ, D_MODEL), F32),
        scratch_shapes=[
            pltpu.VMEM((BATCH, GLA_HEADS, GLA_HV, GLA_HK), F32),
            pltpu.VMEM((BATCH, POOL_HALO + TM_EVEN, POOL_WIDTH), F32),
            pltpu.VMEM((BATCH * TM_EVEN, D_MODEL), F32),
            cat,
            cat,
        ],
        compiler_params=_params("arbitrary"),
        name="even_mixer",
    )(x3, x3, proj.reshape(BATCH, SEQ, EVEN_PROJ), modl,
      wgin, wglr, bglr, normg, poolw, pscale, wout, lng, lnb)
    return out.reshape(TOKENS, D_MODEL)


def _odd_mixer_kernel(x_ref, proj_ref, mod_ref, convw_ref, convb_ref, wrg_ref, brg_ref,
                      wig_ref, big_ref, lam_ref, wout_ref, lng_ref, lnb_ref, o_ref,
                      xbuf_scr, hc_scr, z_scr):
    tm = TM_MIX
    i = pl.program_id(1)

    @pl.when(i == 0)
    def _():
        xbuf_scr[0:CONV_HALO, :] = jnp.zeros((CONV_HALO, LRU_WIDTH), F32)
        hc_scr[...] = jnp.zeros_like(hc_scr)

    xbuf_scr[CONV_HALO:CONV_HALO + tm, :] = proj_ref[:, LRU_WIDTH:2 * LRU_WIDTH]
    pos = lax.broadcasted_iota(jnp.int32, (tm, 1), 0) + i * tm

    sub8 = lax.broadcasted_iota(jnp.int32, (tm, LRU_HD), 0) & 7

    for hb in range(LRU_HEADS):
        c0 = hb * LRU_HD
        xz = xbuf_scr[0:CONV_HALO + tm, c0:c0 + LRU_HD]
        xc = convb_ref[:, c0:c0 + LRU_HD] + convw_ref[CONV_W - 1:CONV_W, c0:c0 + LRU_HD] * xz[CONV_HALO:, :]
        for j in range(1, CONV_W):
            kk = CONV_W - 1 - j
            xc = xc + convw_ref[kk:kk + 1, c0:c0 + LRU_HD] * pltpu.roll(xz, j, 0)[CONV_HALO:, :]
        xcb = xc.astype(BF16)
        rg = _sigmoid(_dot(xcb, wrg_ref[hb]) + brg_ref[:, c0:c0 + LRU_HD])
        ig = _sigmoid(_dot(xcb, wig_ref[hb]) + big_ref[:, c0:c0 + LRU_HD])
        log_a = (-LRU_C) * rg * _softplus(-lam_ref[:, c0:c0 + LRU_HD])
        a = jnp.exp(log_a)
        th = jnp.tanh(log_a)
        m2 = -2.0 * th / (1.0 - th)
        mult = jnp.where(m2 > 0.0, m2 * lax.rsqrt(m2), 0.0)
        mult = jnp.where(pos == 0, 1.0, mult)
        u = (xc * ig) * mult

        for d in (1, 2, 4):
            keep = sub8 >= d
            u = a * jnp.where(keep, pltpu.roll(u, d, 0), 0.0) + u
            a = a * jnp.where(keep, pltpu.roll(a, d, 0), 1.0)
        carry = hc_scr[:, c0:c0 + LRU_HD]
        blocks = []
        for blk in range(tm // 8):
            hb8 = a[blk * 8:(blk + 1) * 8, :] * carry + u[blk * 8:(blk + 1) * 8, :]
            carry = hb8[7:8, :]
            blocks.append(hb8)
        hseq = jnp.concatenate(blocks, axis=0)
        hc_scr[:, c0:c0 + LRU_HD] = carry
        gate = proj_ref[:, c0:c0 + LRU_HD]
        z_scr[:, c0:c0 + LRU_HD] = (_gelu_tanh(gate) * hseq).astype(BF16)

    xbuf_scr[0:CONV_HALO, :] = xbuf_scr[tm:tm + CONV_HALO, :]
    y = _dot(z_scr[...], wout_ref[...])
    m = mod_ref[0]
    o_ref[...] = _ln_residual(x_ref[...], y, m[2:3, :], lng_ref[...], lnb_ref[...])


def _odd_mixer(x2, proj, modl, convw, convb, wrg, brg, wig, big, lam, wout, lng, lnb):
    nt = SEQ // TM_MIX
    tok = lambda b, i: (b * nt + i, 0)
    const2 = lambda b, i: (0, 0)
    const3 = lambda b, i: (0, 0, 0)
    return pl.pallas_call(
        _odd_mixer_kernel,
        grid=(BATCH, nt),
        in_specs=[
            pl.BlockSpec((TM_MIX, D_MODEL), tok),
            pl.BlockSpec((TM_MIX, 2 * LRU_WIDTH), tok),
            pl.BlockSpec((1, 6, D_MODEL), lambda b, i: (b, 0, 0)),
            pl.BlockSpec((CONV_W, LRU_WIDTH), const2),
            pl.BlockSpec((1, LRU_WIDTH), const2),
            pl.BlockSpec((LRU_HEADS, LRU_HD, LRU_HD), const3),
            pl.BlockSpec((1, LRU_WIDTH), const2),
            pl.BlockSpec((LRU_HEADS, LRU_HD, LRU_HD), const3),
            pl.BlockSpec((1, LRU_WIDTH), const2),
            pl.BlockSpec((1, LRU_WIDTH), const2),
            pl.BlockSpec((LRU_WIDTH, D_MODEL), const2),
            pl.BlockSpec((1, D_MODEL), const2),
            pl.BlockSpec((1, D_MODEL), const2),
        ],
        out_specs=pl.BlockSpec((TM_MIX, D_MODEL), tok),
        out_shape=jax.ShapeDtypeStruct((TOKENS, D_MODEL), F32),
        scratch_shapes=[
            pltpu.VMEM((CONV_HALO + TM_MIX, LRU_WIDTH), F32),
            pltpu.VMEM((1, LRU_WIDTH), F32),
            pltpu.VMEM((TM_MIX, LRU_WIDTH), BF16),
        ],
        compiler_params=_params("arbitrary", "arbitrary"),
        name="odd_mixer",
    )(x2, proj, modl, convw, convb, wrg, brg, wig, big, lam, wout, lng, lnb)


def _first_max4(v):
    m = jnp.maximum(jnp.maximum(v[0], v[1]), jnp.maximum(v[2], v[3]))
    idx = jnp.where(v[0] == m, 0, jnp.where(v[1] == m, 1, jnp.where(v[2] == m, 2, 3)))
    return m, idx


def _route(h, wrt_ref, bias_ref):
    h_hi = h.astype(BF16)
    h_lo = (h - h_hi.astype(F32)).astype(BF16)
    w = wrt_ref[...]
    w_hi = w.astype(BF16)
    w_lo = (w - w_hi.astype(F32)).astype(BF16)
    dg = functools.partial(lax.dot_general, dimension_numbers=_NT, preferred_element_type=F32)
    logits = dg(w_hi, h_hi) + dg(w_hi, h_lo) + dg(w_lo, h_hi)
    score = _sigmoid(logits)
    sel = score + bias_ref[...]

    neg = -jnp.inf
    s_rows = [sel[e:e + 1, :] for e in range(N_EXPERTS)]
    group_score, first, second = [], [], []
    for g in range(N_GROUPS):
        v = s_rows[g * EXP_PER_GROUP:(g + 1) * EXP_PER_GROUP]
        m1, i1 = _first_max4(v)
        v2 = [jnp.where(i1 == j, neg, v[j]) for j in range(EXP_PER_GROUP)]
        m2, i2 = _first_max4(v2)
        group_score.append(m1 + m2)
        first.append(i1)
        second.append(i2)
    _, gbest = _first_max4(group_score)
    pick = lambda rows: jnp.where(gbest == 0, rows[0], jnp.where(gbest == 1, rows[1],
                                  jnp.where(gbest == 2, rows[2], rows[3])))
    e1 = gbest * EXP_PER_GROUP + pick(first)
    e2 = gbest * EXP_PER_GROUP + pick(second)
    s1 = jnp.zeros_like(s_rows[0])
    s2 = jnp.zeros_like(s_rows[0])
    for e in range(N_EXPERTS):
        row = score[e:e + 1, :]
        s1 = s1 + jnp.where(e1 == e, row, 0.0)
        s2 = s2 + jnp.where(e2 == e, row, 0.0)
    den = s1 + s2
    return e1, e2, s1 / den, s2 / den


def _router_kernel(x_ref, mod_ref, wrt_ref, bias_ref, hf_ref, eidx_ref, wts_ref):
    m = mod_ref[0]
    h = x_ref[...] * (1.0 + m[4:5, :]) + m[3:4, :]
    hf_ref[...] = h
    e1, e2, w1, w2 = _route(h, wrt_ref, bias_ref)
    eidx_ref[0:1, :] = e1
    eidx_ref[1:2, :] = e2
    wts_ref[0:1, :] = w1
    wts_ref[1:2, :] = w2


def _router(x2, modl, wrt, bias_col):
    nt = SEQ // TM_ROUTE
    tok = lambda b, i: (b * nt + i, 0)
    tokT = lambda b, i: (0, b * nt + i)
    return pl.pallas_call(
        _router_kernel,
        grid=(BATCH, nt),
        in_specs=[
            pl.BlockSpec((TM_ROUTE, D_MODEL), tok),
            pl.BlockSpec((1, 6, D_MODEL), lambda b, i: (b, 0, 0)),
            pl.BlockSpec((N_EXPERTS, D_MODEL), lambda b, i: (0, 0)),
            pl.BlockSpec((N_EXPERTS, 1), lambda b, i: (0, 0)),
        ],
        out_specs=[
            pl.BlockSpec((TM_ROUTE, D_MODEL), tok),
            pl.BlockSpec((2, TM_ROUTE), tokT),
            pl.BlockSpec((2, TM_ROUTE), tokT),
        ],
        out_shape=[
            jax.ShapeDtypeStruct((TOKENS, D_MODEL), F32),
            jax.ShapeDtypeStruct((2, TOKENS), jnp.int32),
            jax.ShapeDtypeStruct((2, TOKENS), F32),
        ],
        compiler_params=_params("arbitrary", "arbitrary"),
        name="moe_router",
    )(x2, modl, wrt, bias_col)


def _prefix_experts(v):
    eidx = lax.broadcasted_iota(jnp.int32, v.shape, 0)
    d = 1
    while d < N_EXPERTS:
        v = v + jnp.where(eidx >= d, pltpu.roll(v, d, 0), 0.0)
        d *= 2
    return v


def _plan_kernel(e_ref, pos_ref, meta_ref, cnt_scr, run_scr, offs_scr):
    tq = TM_ROUTE
    p = pl.program_id(0)
    i = pl.program_id(1)
    e = e_ref[...]
    eid = lax.broadcasted_iota(jnp.int32, (N_EXPERTS, tq), 0)
    oh0 = jnp.where(eid == e[0:1, :], 1.0, 0.0)
    oh1 = jnp.where(eid == e[1:2, :], 1.0, 0.0)

    @pl.when(p == 0)
    def _():
        @pl.when(i == 0)
        def _():
            cnt_scr[...] = jnp.zeros_like(cnt_scr)

        cnt_scr[...] = cnt_scr[...] + jnp.sum(oh0 + oh1, axis=1, keepdims=True)

    @pl.when((p == 1) & (i == 0))
    def _():
        cnt = cnt_scr[...]
        offs = _prefix_experts(cnt) - cnt
        offs_scr[...] = offs
        run_scr[...] = jnp.zeros_like(run_scr)
        nseg = jnp.zeros_like(cnt)
        for k in range(MAX_SEG_PER_EXPERT):
            nseg = nseg + jnp.where(cnt > float(k * SEG_CAP), 1.0, 0.0)
        seg_end = _prefix_experts(nseg)
        seg_begin = seg_end - nseg
        n_seg = jnp.max(seg_end, axis=0, keepdims=True)
        s = lax.broadcasted_iota(jnp.int32, cnt.shape, 1).astype(F32)
        expert = jnp.sum(jnp.where(seg_end <= jnp.minimum(s, n_seg - 1.0), 1.0, 0.0), axis=0, keepdims=True)
        member = jnp.where(seg_begin <= s, jnp.where(s < seg_end, 1.0, 0.0), 0.0)
        done = (s - seg_begin) * float(SEG_CAP)
        rows = jnp.sum(member * jnp.minimum(float(SEG_CAP), cnt - done), axis=0, keepdims=True)
        start = jnp.sum(member * (offs + done), axis=0, keepdims=True)
        meta_ref[...] = jnp.zeros_like(meta_ref)
        meta_ref[0:1, :] = expert.astype(jnp.int32)
        meta_ref[1:2, :] = start.astype(jnp.int32)
        meta_ref[2:3, :] = rows.astype(jnp.int32)
        meta_ref[3:4, :] = n_seg.astype(jnp.int32)

    @pl.when(p == 1)
    def _():
        rj = lax.broadcasted_iota(jnp.int32, (tq, tq), 0)
        ct = lax.broadcasted_iota(jnp.int32, (tq, tq), 1)
        upper = jnp.where(rj <= ct, 1.0, 0.0).astype(BF16)
        pre0 = _dot(oh0.astype(BF16), upper)
        pre1 = _dot(oh1.astype(BF16), upper)
        tot0 = pre0[:, tq - 1:tq]
        tot1 = pre1[:, tq - 1:tq]
        base = offs_scr[:, 0:1] + run_scr[:, 0:1]
        r0 = jnp.sum(oh0 * (base + pre0 - 1.0), axis=0, keepdims=True)
        r1 = jnp.sum(oh1 * (base + tot0 + pre1 - 1.0), axis=0, keepdims=True)
        pos_ref[0:1, :] = r0.astype(jnp.int32)
        pos_ref[1:2, :] = r1.astype(jnp.int32)
        run_scr[...] = run_scr[...] + (tot0 + tot1)


def _plan(eidx):
    nt = TOKENS // TM_ROUTE
    acc = pltpu.VMEM((N_EXPERTS, LANES), F32)
    return pl.pallas_call(
        _plan_kernel,
        grid=(2, nt),
        in_specs=[pl.BlockSpec((2, TM_ROUTE), lambda p, i: (0, i))],
        out_specs=[
            pl.BlockSpec((2, TM_ROUTE), lambda p, i: (0, i * p)),
            pl.BlockSpec((8, LANES), lambda p, i: (0, 0)),
        ],
        out_shape=[
            jax.ShapeDtypeStruct((2, TOKENS), jnp.int32),
            jax.ShapeDtypeStruct((8, LANES), jnp.int32),
        ],
        scratch_shapes=[acc, acc, acc],
        compiler_params=_params("arbitrary", "arbitrary"),
        name="moe_plan",
    )(eidx)


def _invert_kernel(pos_ref, inv_ref):
    base = pl.program_id(0) * INV_CHUNK

    @pl.when(pl.program_id(0) == 0)
    def _():
        for k in range(INV_SPARE):
            inv_ref[2 * TOKENS + k] = 0

    def body(k, carry):
        inv_ref[pos_ref[k]] = base + k
        return carry

    lax.fori_loop(0, INV_CHUNK, body, 0, unroll=8)


def _invert(pos_flat):
    return pl.pallas_call(
        _invert_kernel,
        grid=(2 * TOKENS // INV_CHUNK,),
        in_specs=[pl.BlockSpec((INV_CHUNK,), lambda i: (i,), memory_space=pltpu.SMEM)],
        out_specs=pl.BlockSpec(memory_space=pltpu.SMEM),
        out_shape=jax.ShapeDtypeStruct((2 * TOKENS + INV_SPARE,), jnp.int32),
        compiler_params=_params("arbitrary"),
        name="moe_invert",
    )(pos_flat)


def _row(ref3, row):
    return ref3.at[row >> 3, pl.ds(row & 7, 1), :]


def _wait_rows(n, max_rows, src3, dst3, sem):
    p = 1
    while p <= max_rows:
        @pl.when((n & p) != 0)
        def _(p=p):
            if p < 8:
                cp = pltpu.make_async_copy(src3.at[0, pl.ds(0, p), :], dst3.at[0, pl.ds(0, p), :], sem)
            else:
                cp = pltpu.make_async_copy(src3.at[pl.ds(0, p // 8)], dst3.at[pl.ds(0, p // 8)], sem)
            cp.wait()

        p *= 2


def _ffn_kernel(se_ref, ss_ref, sr_ref, ns_ref, inv_ref, hf_hbm, wg_hbm, wu_hbm, wd_hbm, out_hbm,
                stag, wbf, xb0, xb1, act_scr, ybuf, sem_w, sem_x, sem_y, *, layer):
    s = pl.program_id(0)
    ph = pl.program_id(1)
    n_seg = ns_ref[0]
    e = se_ref[s]
    start = ss_ref[s]
    rows = sr_ref[s]
    n_sub = (rows + (SEG_SUB - 1)) >> SEG_SUB_LOG2
    half = D_MODEL // 2
    hh = D_EXPERT // 2
    nxt = jnp.minimum(s + 1, N_SEG - 1)
    next_start = ss_ref[nxt]
    next_chunks = jnp.where(s + 1 < n_seg, ((sr_ref[nxt] + (SEG_SUB - 1)) >> SEG_SUB_LOG2) * CHUNKS_PER_SUB, 0)

    def weight_copies(phase, expert, slot):
        if phase < 2:
            cols = pl.ds(phase * hh, hh)
            return [pltpu.make_async_copy(w_hbm.at[layer, expert, :, cols],
                                          stag.at[slot, :, pl.ds(c * hh, hh)], sem_w.at[slot])
                    for c, w_hbm in enumerate((wg_hbm, wu_hbm))]
        return [pltpu.make_async_copy(wd_hbm.at[layer, expert, :, pl.ds(c * half, half)],
                                      stag.at[slot, pl.ds(c * D_EXPERT, D_EXPERT), :], sem_w.at[slot])
                for c in range(2)]

    def gather_chunk(seg_start, c, xdst, sem):
        base = seg_start + c * GATHER_CHUNK
        g0 = c * (GATHER_CHUNK // 8)
        for i in range(GATHER_CHUNK // 8):
            for j in range(8):
                tok = inv_ref[base + (i * 8 + j)] & (TOKENS - 1)
                pltpu.make_async_copy(hf_hbm.at[pl.ds(tok, 1), :], xdst.at[g0 + i, pl.ds(j, 1), :], sem).start()

    blk = D_MODEL // W_SLICES

    def wait_weights(phase, slot):
        for cp in weight_copies(phase, e, slot):
            cp.wait()

    def cast_slice(kk, slot):
        w = stag[slot, kk * blk:(kk + 1) * blk, :].astype(BF16)
        wbf[kk * blk:(kk + 1) * blk, :] = w
        return w

    def gate_up_half(phase, slot, xcur, xnext, sem_next):
        first_chunk = phase * n_sub
        with_gather = jnp.clip(next_chunks - first_chunk, 0, n_sub)

        def finish(k, gu, gather):
            r0 = pl.multiple_of(k * SEG_SUB, SEG_SUB)
            g = gu[:, 0:hh]
            act_scr[pl.ds(r0, SEG_SUB), phase * hh:(phase + 1) * hh] = (
                (g * _sigmoid(g)) * gu[:, hh:2 * hh]).astype(BF16)
            if gather:
                gather_chunk(next_start, first_chunk + k, xnext, sem_next)

        def block(k, gather):
            x = xcur[pl.ds(k * (SEG_SUB // 8), SEG_SUB // 8)].reshape(SEG_SUB, D_MODEL)
            finish(k, _dot(x.astype(BF16), wbf[...]), gather)

        def first_block(gather):
            x = xcur[pl.ds(0, SEG_SUB // 8)].reshape(SEG_SUB, D_MODEL)
            gu = None
            for kk in range(W_SLICES):
                part = _dot(x[:, kk * blk:(kk + 1) * blk].astype(BF16), cast_slice(kk, slot))
                gu = part if gu is None else gu + part
            finish(0, gu, gather)

        def gathering(k, carry):
            block(k, True)
            return carry

        def plain(k, carry):
            block(k, False)
            return carry

        @pl.when(with_gather > 0)
        def _():
            first_block(True)

        @pl.when(with_gather == 0)
        def _():
            first_block(False)

        lax.fori_loop(1, jnp.maximum(with_gather, 1), gathering, 0)
        lax.fori_loop(jnp.maximum(with_gather, 1), n_sub, plain, 0)

    def leftover_chunks(seg_start, first, total, xdst, sem):
        def body(c, carry):
            gather_chunk(seg_start, c, xdst, sem)
            return carry

        lax.fori_loop(first, total, body, 0)

    def wait_gathered(xcur, sem):
        _wait_rows(n_sub * SEG_SUB, SEG_CAP, xcur, xcur, sem)

    valid = s < n_seg
    q = 3 * s + ph
    slot = q % 2

    def phase0(xcur, xnext, sem_cur, sem_next):
        @pl.when(s == 0)
        def _():
            xb0[...] = jnp.zeros_like(xb0)
            xb1[...] = jnp.zeros_like(xb1)
            for cp in weight_copies(0, e, slot):
                cp.start()
            leftover_chunks(start, 0, n_sub * CHUNKS_PER_SUB, xcur, sem_cur)

        for cp in weight_copies(1, e, 1 - slot):
            cp.start()
        wait_gathered(xcur, sem_cur)
        wait_weights(0, slot)
        gate_up_half(0, slot, xcur, xnext, sem_next)

    def phase1(xcur, xnext, sem_next):
        for cp in weight_copies(2, e, 1 - slot):
            cp.start()
        wait_weights(1, slot)
        gate_up_half(1, slot, xcur, xnext, sem_next)
        leftover_chunks(next_start, 2 * n_sub, next_chunks, xnext, sem_next)

    even = (s % 2) == 0

    @pl.when(valid & (ph == 0) & even)
    def _():
        phase0(xb0, xb1, sem_x.at[0], sem_x.at[1])

    @pl.when(valid & (ph == 0) & jnp.logical_not(even))
    def _():
        phase0(xb1, xb0, sem_x.at[1], sem_x.at[0])

    @pl.when(valid & (ph == 1) & even)
    def _():
        phase1(xb0, xb1, sem_x.at[1])

    @pl.when(valid & (ph == 1) & jnp.logical_not(even))
    def _():
        phase1(xb1, xb0, sem_x.at[0])

    @pl.when(valid & (ph == 2))
    def _():
        @pl.when(s + 1 < n_seg)
        def _():
            for cp in weight_copies(0, se_ref[s + 1], 1 - slot):
                cp.start()

        wait_weights(2, slot)

        def out_row(pair):
            return out_hbm.at[pl.ds(pair, 1), :]

        def down(k):
            r0 = pl.multiple_of(k * SEG_SUB, SEG_SUB)
            act = act_scr[pl.ds(r0, SEG_SUB), :]
            for c in range(2):
                ybuf[k % 2, :, :, c * half:(c + 1) * half] = _dot(
                    act, wbf[c * D_EXPERT:(c + 1) * D_EXPERT, :]).reshape(SEG_SUB // 8, 8, half)

        def scatter_full(k):
            base = start + k * SEG_SUB
            for i in range(SEG_SUB // 8):
                for j in range(8):
                    pltpu.make_async_copy(ybuf.at[k % 2, i, pl.ds(j, 1), :], out_row(inv_ref[base + (i * 8 + j)]),
                                          sem_y.at[k % 2]).start(priority=1)

        def wait_scattered(n, k):
            _wait_rows(n, SEG_SUB, ybuf.at[k % 2], ybuf.at[k % 2], sem_y.at[k % 2])

        def drain(seg_rows):
            subs = (seg_rows + (SEG_SUB - 1)) >> SEG_SUB_LOG2

            @pl.when(subs >= 2)
            def _():
                wait_scattered(SEG_SUB, subs)
            wait_scattered(seg_rows - (subs - 1) * SEG_SUB, subs - 1)

        @pl.when(s > 0)
        def _():
            drain(sr_ref[jnp.maximum(s - 1, 0)])

        def down_first():
            act = act_scr[0:SEG_SUB, :]
            per_half = W_SLICES // 2
            for c in range(2):
                acc = None
                for kk in range(per_half):
                    part = _dot(act[:, kk * blk:(kk + 1) * blk], cast_slice(c * per_half + kk, slot))
                    acc = part if acc is None else acc + part
                ybuf[0, :, :, c * half:(c + 1) * half] = acc.reshape(SEG_SUB // 8, 8, half)

        down_first()

        @pl.when(n_sub >= 2)
        def _():
            scatter_full(0)
            down(1)

        def steady(k, carry):
            wait_scattered(SEG_SUB, k)
            scatter_full(k - 1)
            down(k)
            return carry

        lax.fori_loop(2, n_sub, steady, 0)

        last = n_sub - 1
        n_rows = rows - last * SEG_SUB
        base = start + last * SEG_SUB

        def group(i, c):
            for j in range(8):
                pltpu.make_async_copy(ybuf.at[last % 2, i, pl.ds(j, 1), :], out_row(inv_ref[base + i * 8 + j]),
                                      sem_y.at[last % 2]).start(priority=1)
            return c

        lax.fori_loop(0, n_rows >> 3, group, 0)

        def single(r, c):
            pltpu.make_async_copy(_row(ybuf.at[last % 2], r), out_row(inv_ref[base + r]),
                                  sem_y.at[last % 2]).start(priority=1)
            return c

        lax.fori_loop((n_rows >> 3) << 3, n_rows, single, 0)

        @pl.when(s == n_seg - 1)
        def _():
            drain(rows)


def _ffn(layer, seg_expert, seg_start, seg_rows, n_seg, inv, hf, wg, wu, wd):
    assert D_EXPERT * 2 == D_MODEL
    anyspec = pl.BlockSpec(memory_space=pl.ANY)
    grid_spec = pltpu.PrefetchScalarGridSpec(
        num_scalar_prefetch=5,
        grid=(N_SEG, 3),
        in_specs=[anyspec, anyspec, anyspec, anyspec],
        out_specs=anyspec,
        scratch_shapes=[
            pltpu.VMEM((2, D_MODEL, D_EXPERT), F32),
            pltpu.VMEM((D_MODEL, D_EXPERT), BF16),
            pltpu.VMEM((SEG_CAP // 8, 8, D_MODEL), F32),
            pltpu.VMEM((SEG_CAP // 8, 8, D_MODEL), F32),
            pltpu.VMEM((SEG_CAP, D_EXPERT), BF16),
            pltpu.VMEM((2, SEG_SUB // 8, 8, D_MODEL), F32),
            pltpu.SemaphoreType.DMA((2,)),
            pltpu.SemaphoreType.DMA((2,)),
            pltpu.SemaphoreType.DMA((2,)),
        ],
    )
    return pl.pallas_call(
        functools.partial(_ffn_kernel, layer=layer),
        grid_spec=grid_spec,
        out_shape=jax.ShapeDtypeStruct((2 * TOKENS, D_MODEL), F32),
        compiler_params=_params("arbitrary", "arbitrary"),
        name="moe_ffn",
    )(seg_expert, seg_start, seg_rows, n_seg, inv, hf, wg, wu, wd)


def _combine_kernel(x_ref, mod_ref, w_ref, lng_ref, lnb_ref, y0_ref, y1_ref, o_ref):
    w = w_ref[...]
    y = w[:, 0:1] * y0_ref[...] + w[:, 1:2] * y1_ref[...]
    m = mod_ref[0]
    o_ref[...] = _ln_residual(x_ref[...], y, m[5:6, :], lng_ref[...], lnb_ref[...])


def _combine(x2, modl, wts_col, lng, lnb, y2):
    nt = SEQ // TM_COMB
    tok = lambda b, i: (b * nt + i, 0)
    tok1 = lambda b, i: (TOKENS // TM_COMB + b * nt + i, 0)
    return pl.pallas_call(
        _combine_kernel,
        grid=(BATCH, nt),
        in_specs=[
            pl.BlockSpec((TM_COMB, D_MODEL), tok),
            pl.BlockSpec((1, 6, D_MODEL), lambda b, i: (b, 0, 0)),
            pl.BlockSpec((TM_COMB, 2), tok),
            pl.BlockSpec((1, D_MODEL), lambda b, i: (0, 0)),
            pl.BlockSpec((1, D_MODEL), lambda b, i: (0, 0)),
            pl.BlockSpec((TM_COMB, D_MODEL), tok),
            pl.BlockSpec((TM_COMB, D_MODEL), tok1),
        ],
        out_specs=pl.BlockSpec((TM_COMB, D_MODEL), tok),
        out_shape=jax.ShapeDtypeStruct((TOKENS, D_MODEL), F32),
        compiler_params=_params("arbitrary", "arbitrary"),
        name="moe_combine",
    )(x2, modl, wts_col, lng, lnb, y2, y2)


def _moe(layer, x2, modl, wrt, bias_col, wg, wu, wd, lng, lnb):
    hf, eidx, wts = _router(x2, modl, wrt, bias_col)
    pos, meta = _plan(eidx)
    inv = _invert(pos.reshape(-1))
    y2 = _ffn(layer, meta[0, :N_SEG], meta[1, :N_SEG], meta[2, :N_SEG], meta[3, :1], inv, hf, wg, wu, wd)
    return _combine(x2, modl, wts.T, lng, lnb, y2)


def kernel(x, c, w_ada, b_ada, ln_mix_g, ln_mix_b, ln_ffn_g, ln_ffn_b, even_w_in, gla_w_glr, gla_b_glr, gla_norm_g, pool_w, pool_scale, even_w_out, odd_w_in, lru_conv_w, lru_conv_b, lru_w_rg, lru_b_rg, lru_w_ig, lru_b_ig, lru_lambda, odd_w_out, w_router, router_bias, exp_w_gate, exp_w_up, exp_w_down):
    mod = _adaln_mod(c, w_ada, b_ada)
    x2 = x.reshape(TOKENS, D_MODEL)
    wrt = w_router.T
    bias_col = router_bias.reshape(N_EXPERTS, 1)
    row = lambda a: a.reshape(1, -1)
    for l in range(DEPTH):
        j = l // 2
        modl = mod[l]
        if l % 2 == 0:
            w_in = even_w_in[j]
            w_main = jnp.concatenate([w_in[:, :SPLIT_R], w_in[:, SPLIT_G:]], axis=1).astype(BF16)
            wgin = jnp.pad(w_in[:, SPLIT_R:SPLIT_G], ((0, 0), (0, LANES - GLA_RANK))).astype(BF16)
            wglr = jnp.pad(gla_w_glr[j], ((0, LANES - GLA_RANK), (0, 0))).astype(BF16)
            proj = _inproj(x2, modl, w_main)
            x2 = _even_mixer(x2, proj, modl, wgin, wglr, row(gla_b_glr[j]), row(gla_norm_g[j]),
                             pool_w[j].astype(BF16), row(pool_scale[j]), even_w_out[j].astype(BF16),
                             row(ln_mix_g[l]), row(ln_mix_b[l]))
        else:
            proj = _inproj(x2, modl, odd_w_in[j].astype(BF16))
            x2 = _odd_mixer(x2, proj, modl, lru_conv_w[j], row(lru_conv_b[j]),
                            lru_w_rg[j].astype(BF16), row(lru_b_rg[j]), lru_w_ig[j].astype(BF16),
                            row(lru_b_ig[j]), row(lru_lambda[j]), odd_w_out[j].astype(BF16),
                            row(ln_mix_g[l]), row(ln_mix_b[l]))
        x2 = _moe(l, x2, modl, wrt, bias_col, exp_w_gate, exp_w_up, exp_w_down,
                  row(ln_ffn_g[l]), row(ln_ffn_b[l]))
    return x2.reshape(BATCH, SEQ, D_MODEL)
```

```python
import functools

import jax
import jax.numpy as jnp
from jax import lax
from jax.experimental import pallas as pl
from jax.experimental.pallas import tpu as pltpu

F32 = jnp.float32
BF16 = jnp.bfloat16

D_MODEL = 2048
BATCH = 2
SEQ = 4096
TOKENS = BATCH * SEQ
DEPTH = 4
ALPHA = (2 * DEPTH) ** 0.25
LN_EPS = 1e-5

GLA_HEADS = 4
GLA_DK = D_MODEL // 4
GLA_DV = D_MODEL // 2
GLA_HK = GLA_DK // GLA_HEADS
GLA_HV = GLA_DV // GLA_HEADS
GLA_RANK = 16
GLA_TAU = 16.0
GLA_CHUNK = 64
POOL_WIDTH = D_MODEL // 2
POOL_WINDOWS = (2, 4, 8, 16)
POOL_GC = POOL_WIDTH // len(POOL_WINDOWS)
POOL_HALO = 16
SPLIT_R = 2 * GLA_DK + 2 * GLA_DV
SPLIT_G = SPLIT_R + GLA_RANK
EVEN_PROJ = SPLIT_R + POOL_WIDTH

LRU_WIDTH = 5 * D_MODEL // 4
LRU_HEADS = 10
LRU_HD = LRU_WIDTH // LRU_HEADS
CONV_W = 4
CONV_HALO = 8
LRU_C = 8.0

N_EXPERTS = 16
N_GROUPS = 4
EXP_PER_GROUP = N_EXPERTS // N_GROUPS
D_EXPERT = D_MODEL // 2

LANES = 128
VMEM_LIMIT = 56 * 1024 * 1024

TM_PROJ = 1024
TN_PROJ = 1024
TM_MIX = 256
TM_EVEN = 128
OUT_SLAB = 256
OUT_SLAB_EVERY = (BATCH * (TM_EVEN // GLA_CHUNK) * GLA_HEADS) // (D_MODEL // OUT_SLAB)
TM_ROUTE = 512
TM_COMB = 256
SEG_SUB_LOG2 = 8
SEG_SUB = 1 << SEG_SUB_LOG2
GATHER_CHUNK = 128
CHUNKS_PER_SUB = SEG_SUB // GATHER_CHUNK
SEG_CAP = 5 * SEG_SUB
MAX_SEG_PER_EXPERT = -(-TOKENS // SEG_CAP)
N_SEG = N_EXPERTS + (2 * TOKENS) // SEG_CAP + 1
W_SLICES = 8
WEIGHT_DMA_PRIORITY = 1
INV_CHUNK = 2048
INV_SPARE = SEG_SUB

_NT = (((1,), (1,)), ((), ()))


def _sigmoid(x):
    return 1.0 / (1.0 + jnp.exp(-x))


def _softplus(x):
    return jnp.maximum(x, 0.0) + jnp.log1p(jnp.exp(-jnp.abs(x)))


def _gelu_tanh(x):
    return 0.5 * x * (1.0 + jnp.tanh(0.7978845608028654 * (x + 0.044715 * (x * x * x))))


def _dot(a, b):
    return jnp.dot(a, b, preferred_element_type=F32)


def _ln_residual(x, y, gate, ln_g, ln_b):
    r = ALPHA * x + (1.0 + gate) * y
    mu = jnp.mean(r, axis=-1, keepdims=True)
    d = r - mu
    var = jnp.mean(d * d, axis=-1, keepdims=True)
    return d * lax.rsqrt(var + LN_EPS) * ln_g + ln_b


def _params(*sem):
    return pltpu.CompilerParams(dimension_semantics=sem, vmem_limit_bytes=VMEM_LIMIT)


def _mod_kernel(c_ref, w_ref, b_ref, o_ref):
    c = c_ref[...]
    sc = (c * _sigmoid(c)).astype(BF16)
    o_ref[0] = _dot(sc, w_ref[0].astype(BF16)) + b_ref[0]


def _adaln_mod(c, w_ada, b_ada):
    tn = 1024
    n = 6 * D_MODEL
    c8 = jnp.pad(c, ((0, 8 - BATCH), (0, 0)))
    out = pl.pallas_call(
        _mod_kernel,
        grid=(DEPTH, n // tn),
        in_specs=[
            pl.BlockSpec((8, D_MODEL), lambda l, j: (0, 0)),
            pl.BlockSpec((1, D_MODEL, tn), lambda l, j: (l, 0, j)),
            pl.BlockSpec((1, 1, tn), lambda l, j: (l, 0, j)),
        ],
        out_specs=pl.BlockSpec((1, 8, tn), lambda l, j: (l, 0, j)),
        out_shape=jax.ShapeDtypeStruct((DEPTH, 8, n), F32),
        compiler_params=_params("arbitrary", "arbitrary"),
        name="adaln_mod",
    )(c8, w_ada, b_ada.reshape(DEPTH, 1, n))
    return out[:, :BATCH, :].reshape(DEPTH, BATCH, 6, D_MODEL)


def _inproj_kernel(x_ref, mod_ref, w_ref, o_ref, h_scr):
    @pl.when(pl.program_id(2) == 0)
    def _():
        m = mod_ref[0]
        h_scr[...] = (x_ref[...] * (1.0 + m[1:2, :]) + m[0:1, :]).astype(BF16)

    o_ref[...] = _dot(h_scr[...], w_ref[...])


def _inproj(x2, modl, w_bf16):
    n = w_bf16.shape[1]
    ntm = SEQ // TM_PROJ
    return pl.pallas_call(
        _inproj_kernel,
        grid=(BATCH, ntm, n // TN_PROJ),
        in_specs=[
            pl.BlockSpec((TM_PROJ, D_MODEL), lambda b, i, j: (b * ntm + i, 0)),
            pl.BlockSpec((1, 6, D_MODEL), lambda b, i, j: (b, 0, 0)),
            pl.BlockSpec((D_MODEL, TN_PROJ), lambda b, i, j: (0, j)),
        ],
        out_specs=pl.BlockSpec((TM_PROJ, TN_PROJ), lambda b, i, j: (b * ntm + i, j)),
        out_shape=jax.ShapeDtypeStruct((TOKENS, n), F32),
        scratch_shapes=[pltpu.VMEM((TM_PROJ, D_MODEL), BF16)],
        compiler_params=_params("arbitrary", "arbitrary", "arbitrary"),
        name="inproj",
    )(x2, modl, w_bf16)


def _even_mixer_kernel(x_ref, xprev_ref, proj_ref, mod_ref, wgin_ref, wglr_ref, bglr_ref, normg_ref,
                       poolw_ref, pscale_ref, wout_ref, lng_ref, lnb_ref, o_ref,
                       st_scr, ubuf_scr, y_scr, cat0_scr, cat1_scr):
    i = pl.program_id(0)

    @pl.when(i == 0)
    def _():
        st_scr[...] = jnp.zeros_like(st_scr)
        ubuf_scr[:, 0:POOL_HALO, :] = jnp.zeros((BATCH, POOL_HALO, POOL_WIDTH), F32)
        cat1_scr[...] = jnp.zeros_like(cat1_scr)

    refs = (x_ref, xprev_ref, proj_ref, mod_ref, wgin_ref, wglr_ref, bglr_ref, normg_ref,
            poolw_ref, pscale_ref, wout_ref, lng_ref, lnb_ref, o_ref, st_scr, ubuf_scr, y_scr)

    @pl.when(i % 2 == 0)
    def _():
        _even_mixer_step(i, *refs, cat_scr=cat0_scr, cat_prev=cat1_scr)

    @pl.when(i % 2 == 1)
    def _():
        _even_mixer_step(i, *refs, cat_scr=cat1_scr, cat_prev=cat0_scr)


def _even_mixer_step(i, x_ref, xprev_ref, proj_ref, mod_ref, wgin_ref, wglr_ref, bglr_ref, normg_ref,
                     poolw_ref, pscale_ref, wout_ref, lng_ref, lnb_ref, o_ref, st_scr, ubuf_scr, y_scr,
                     *, cat_scr, cat_prev):
    tb = TM_EVEN
    tm = BATCH * tb
    ms = [mod_ref[b] for b in range(BATCH)]

    h = jnp.concatenate([(x_ref[b] * (1.0 + ms[b][1:2, :]) + ms[b][0:1, :]).astype(BF16)
                         for b in range(BATCH)], axis=0)

    glr = _dot(h, wgin_ref[...])
    z = _dot(glr.astype(BF16), wglr_ref[...]) + bglr_ref[...]
    log_a = -_softplus(-z) * (1.0 / GLA_TAU)

    la_hi = log_a.astype(BF16)
    res = log_a - la_hi.astype(F32)
    la_mid = res.astype(BF16)
    la_lo = (res - la_mid.astype(F32)).astype(BF16)
    ri = lax.broadcasted_iota(jnp.int32, (tm, tm), 0)
    ci = lax.broadcasted_iota(jnp.int32, (tm, tm), 1)
    back = ri - ci
    tril = jnp.where(back >= 0, jnp.where(back <= (ri & (GLA_CHUNK - 1)), 1.0, 0.0), 0.0).astype(BF16)
    bcum = _dot(tril, la_hi) + _dot(tril, la_mid) + _dot(tril, la_lo)

    r64 = lax.broadcasted_iota(jnp.int32, (GLA_CHUNK, GLA_CHUNK), 0)
    c64 = lax.broadcasted_iota(jnp.int32, (GLA_CHUNK, GLA_CHUNK), 1)
    causal = c64 <= r64
    tn_dims = (((0,), (0,)), ((), ()))

    for bi in range(BATCH):
        for c in range(tb // GLA_CHUNK):
            r0 = c * GLA_CHUNK
            j0 = bi * tb + r0
            for hh in range(GLA_HEADS):
                k0 = hh * GLA_HK
                v0 = hh * GLA_HV
                b = bcum[j0:j0 + GLA_CHUNK, k0:k0 + GLA_HK]
                b_last = b[GLA_CHUNK - 1:GLA_CHUNK, :]
                q = proj_ref[bi, r0:r0 + GLA_CHUNK, k0:k0 + GLA_HK]
                k = proj_ref[bi, r0:r0 + GLA_CHUNK, GLA_DK + k0:GLA_DK + k0 + GLA_HK]
                v = proj_ref[bi, r0:r0 + GLA_CHUNK, 2 * GLA_DK + v0:2 * GLA_DK + v0 + GLA_HV].astype(BF16)
                q_in = ((q * (GLA_HK ** -0.5)) * jnp.exp(b)).astype(BF16)
                k_in = (k * jnp.exp(-b)).astype(BF16)
                k_out = (k * jnp.exp(b_last - b)).astype(BF16)
                scores = lax.dot_general(q_in, k_in, _NT, preferred_element_type=F32)
                scores = jnp.where(causal, scores, 0.0).astype(BF16)
                st = st_scr[bi, hh]
                o = _dot(scores, v) + lax.dot_general(q_in, st.astype(BF16), _NT, preferred_element_type=F32)
                kv_t = lax.dot_general(v, k_out, tn_dims, preferred_element_type=F32)
                st_scr[bi, hh] = st * jnp.exp(b_last) + kv_t
                msq = jnp.mean(o * o, axis=-1, keepdims=True)
                on = o * lax.rsqrt(msq + LN_EPS) * normg_ref[:, v0:v0 + GLA_HV]
                r = proj_ref[bi, r0:r0 + GLA_CHUNK, SPLIT_R - GLA_DV + v0:SPLIT_R - GLA_DV + v0 + GLA_HV]
                cat_scr[j0:j0 + GLA_CHUNK, v0:v0 + GLA_HV] = (on * (r * _sigmoid(r))).astype(BF16)
                idx = (bi * (tb // GLA_CHUNK) + c) * GLA_HEADS + hh
                if idx % OUT_SLAB_EVERY == OUT_SLAB_EVERY - 1:
                    s0 = (idx // OUT_SLAB_EVERY) * OUT_SLAB
                    y_scr[:, s0:s0 + OUT_SLAB] = _dot(cat_prev[...], wout_ref[:, s0:s0 + OUT_SLAB])

    pos = lax.broadcasted_iota(jnp.int32, (tb, 1), 0) + jnp.minimum(i, SEQ // tb - 1) * tb
    for bi in range(BATCH):
        ubuf_scr[bi, POOL_HALO:POOL_HALO + tb, :] = proj_ref[bi, :, SPLIT_R:SPLIT_R + POOL_WIDTH]
    for g, w in enumerate(POOL_WINDOWS):
        c0 = g * POOL_GC
        cnt = jnp.minimum(pos + 1, w).astype(F32)
        ps = []
        for bi in range(BATCH):
            cur = ubuf_scr[bi, POOL_HALO:POOL_HALO + tb, c0:c0 + POOL_GC]
            acc = cur
            for kk in range(1, w):
                acc = acc + ubuf_scr[bi, POOL_HALO - kk:POOL_HALO - kk + tb, c0:c0 + POOL_GC]
            ps.append((acc / cnt - cur).astype(BF16))
        pp = _dot(jnp.concatenate(ps, axis=0), poolw_ref[g]) * pscale_ref[:, c0:c0 + POOL_GC]
        cat_scr[:, GLA_DV + c0:GLA_DV + c0 + POOL_GC] = pp.astype(BF16)
    for bi in range(BATCH):
        ubuf_scr[bi, 0:POOL_HALO, :] = ubuf_scr[bi, tb:tb + POOL_HALO, :]

    for bi in range(BATCH):
        o_ref[bi] = _ln_residual(xprev_ref[bi], y_scr[bi * tb:(bi + 1) * tb, :], ms[bi][2:3, :],
                                 lng_ref[...], lnb_ref[...])


def _even_mixer(x2, proj, modl, wgin, wglr, bglr, normg, poolw, pscale, wout, lng, lnb):
    nt = SEQ // TM_EVEN
    cur = lambda i: (0, jnp.minimum(i, nt - 1), 0)
    prev = lambda i: (0, jnp.maximum(i - 1, 0), 0)
    const2 = lambda i: (0, 0)
    const3 = lambda i: (0, 0, 0)
    x3 = x2.reshape(BATCH, SEQ, D_MODEL)
    cat = pltpu.VMEM((BATCH * TM_EVEN, D_MODEL), BF16)
    out = pl.pallas_call(
        _even_mixer_kernel,
        grid=(nt + 1,),
        in_specs=[
            pl.BlockSpec((BATCH, TM_EVEN, D_MODEL), cur),
            pl.BlockSpec((BATCH, TM_EVEN, D_MODEL), prev),
            pl.BlockSpec((BATCH, TM_EVEN, EVEN_PROJ), cur),
            pl.BlockSpec((BATCH, 6, D_MODEL), const3),
            pl.BlockSpec((D_MODEL, LANES), const2),
            pl.BlockSpec((LANES, GLA_DK), const2),
            pl.BlockSpec((1, GLA_DK), const2),
            pl.BlockSpec((1, GLA_DV), const2),
            pl.BlockSpec((len(POOL_WINDOWS), POOL_GC, POOL_GC), const3),
            pl.BlockSpec((1, POOL_WIDTH), const2),
            pl.BlockSpec((D_MODEL, D_MODEL), const2),
            pl.BlockSpec((1, D_MODEL), const2),
            pl.BlockSpec((1, D_MODEL), const2),
        ],
        out_specs=pl.BlockSpec((BATCH, TM_EVEN, D_MODEL), prev),
        out_shape=jax.ShapeDtypeStruct((BATCH, SEQ, D_MODEL), F32),
        scratch_shapes=[
            pltpu.VMEM((BATCH, GLA_HEADS, GLA_HV, GLA_HK), F32),
            pltpu.VMEM((BATCH, POOL_HALO + TM_EVEN, POOL_WIDTH), F32),
            pltpu.VMEM((BATCH * TM_EVEN, D_MODEL), F32),
            cat,
            cat,
        ],
        compiler_params=_params("arbitrary"),
        name="even_mixer",
    )(x3, x3, proj.reshape(BATCH, SEQ, EVEN_PROJ), modl,
      wgin, wglr, bglr, normg, poolw, pscale, wout, lng, lnb)
    return out.reshape(TOKENS, D_MODEL)


def _odd_mixer_kernel(x_ref, proj_ref, mod_ref, convw_ref, convb_ref, wrg_ref, brg_ref,
                      wig_ref, big_ref, lam_ref, wout_ref, lng_ref, lnb_ref, o_ref,
                      xbuf_scr, hc_scr, z_scr):
    tm = TM_MIX
    i = pl.program_id(1)

    @pl.when(i == 0)
    def _():
        xbuf_scr[0:CONV_HALO, :] = jnp.zeros((CONV_HALO, LRU_WIDTH), F32)
        hc_scr[...] = jnp.zeros_like(hc_scr)

    xbuf_scr[CONV_HALO:CONV_HALO + tm, :] = proj_ref[:, LRU_WIDTH:2 * LRU_WIDTH]
    pos = lax.broadcasted_iota(jnp.int32, (tm, 1), 0) + i * tm

    sub8 = lax.broadcasted_iota(jnp.int32, (tm, LRU_HD), 0) & 7

    for hb in range(LRU_HEADS):
        c0 = hb * LRU_HD
        xz = xbuf_scr[0:CONV_HALO + tm, c0:c0 + LRU_HD]
        xc = convb_ref[:, c0:c0 + LRU_HD] + convw_ref[CONV_W - 1:CONV_W, c0:c0 + LRU_HD] * xz[CONV_HALO:, :]
        for j in range(1, CONV_W):
            kk = CONV_W - 1 - j
            xc = xc + convw_ref[kk:kk + 1, c0:c0 + LRU_HD] * pltpu.roll(xz, j, 0)[CONV_HALO:, :]
        xcb = xc.astype(BF16)
        rg = _sigmoid(_dot(xcb, wrg_ref[hb]) + brg_ref[:, c0:c0 + LRU_HD])
        ig = _sigmoid(_dot(xcb, wig_ref[hb]) + big_ref[:, c0:c0 + LRU_HD])
        log_a = (-LRU_C) * rg * _softplus(-lam_ref[:, c0:c0 + LRU_HD])
        a = jnp.exp(log_a)
        th = jnp.tanh(log_a)
        m2 = -2.0 * th / (1.0 - th)
        mult = jnp.where(m2 > 0.0, m2 * lax.rsqrt(m2), 0.0)
        mult = jnp.where(pos == 0, 1.0, mult)
        u = (xc * ig) * mult

        for d in (1, 2, 4):
            keep = sub8 >= d
            u = a * jnp.where(keep, pltpu.roll(u, d, 0), 0.0) + u
            a = a * jnp.where(keep, pltpu.roll(a, d, 0), 1.0)
        carry = hc_scr[:, c0:c0 + LRU_HD]
        blocks = []
        for blk in range(tm // 8):
            hb8 = a[blk * 8:(blk + 1) * 8, :] * carry + u[blk * 8:(blk + 1) * 8, :]
            carry = hb8[7:8, :]
            blocks.append(hb8)
        hseq = jnp.concatenate(blocks, axis=0)
        hc_scr[:, c0:c0 + LRU_HD] = carry
        gate = proj_ref[:, c0:c0 + LRU_HD]
        z_scr[:, c0:c0 + LRU_HD] = (_gelu_tanh(gate) * hseq).astype(BF16)

    xbuf_scr[0:CONV_HALO, :] = xbuf_scr[tm:tm + CONV_HALO, :]
    y = _dot(z_scr[...], wout_ref[...])
    m = mod_ref[0]
    o_ref[...] = _ln_residual(x_ref[...], y, m[2:3, :], lng_ref[...], lnb_ref[...])


def _odd_mixer(x2, proj, modl, convw, convb, wrg, brg, wig, big, lam, wout, lng, lnb):
    nt = SEQ // TM_MIX
    tok = lambda b, i: (b * nt + i, 0)
    const2 = lambda b, i: (0, 0)
    const3 = lambda b, i: (0, 0, 0)
    return pl.pallas_call(
        _odd_mixer_kernel,
        grid=(BATCH, nt),
        in_specs=[
            pl.BlockSpec((TM_MIX, D_MODEL), tok),
            pl.BlockSpec((TM_MIX, 2 * LRU_WIDTH), tok),
            pl.BlockSpec((1, 6, D_MODEL), lambda b, i: (b, 0, 0)),
            pl.BlockSpec((CONV_W, LRU_WIDTH), const2),
            pl.BlockSpec((1, LRU_WIDTH), const2),
            pl.BlockSpec((LRU_HEADS, LRU_HD, LRU_HD), const3),
            pl.BlockSpec((1, LRU_WIDTH), const2),
            pl.BlockSpec((LRU_HEADS, LRU_HD, LRU_HD), const3),
            pl.BlockSpec((1, LRU_WIDTH), const2),
            pl.BlockSpec((1, LRU_WIDTH), const2),
            pl.BlockSpec((LRU_WIDTH, D_MODEL), const2),
            pl.BlockSpec((1, D_MODEL), const2),
            pl.BlockSpec((1, D_MODEL), const2),
        ],
        out_specs=pl.BlockSpec((TM_MIX, D_MODEL), tok),
        out_shape=jax.ShapeDtypeStruct((TOKENS, D_MODEL), F32),
        scratch_shapes=[
            pltpu.VMEM((CONV_HALO + TM_MIX, LRU_WIDTH), F32),
            pltpu.VMEM((1, LRU_WIDTH), F32),
            pltpu.VMEM((TM_MIX, LRU_WIDTH), BF16),
        ],
        compiler_params=_params("arbitrary", "arbitrary"),
        name="odd_mixer",
    )(x2, proj, modl, convw, convb, wrg, brg, wig, big, lam, wout, lng, lnb)


def _first_max4(v):
    m = jnp.maximum(jnp.maximum(v[0], v[1]), jnp.maximum(v[2], v[3]))
    idx = jnp.where(v[0] == m, 0, jnp.where(v[1] == m, 1, jnp.where(v[2] == m, 2, 3)))
    return m, idx


def _route(h, wrt_ref, bias_ref):
    h_hi = h.astype(BF16)
    h_lo = (h - h_hi.astype(F32)).astype(BF16)
    w = wrt_ref[...]
    w_hi = w.astype(BF16)
    w_lo = (w - w_hi.astype(F32)).astype(BF16)
    dg = functools.partial(lax.dot_general, dimension_numbers=_NT, preferred_element_type=F32)
    logits = dg(w_hi, h_hi) + dg(w_hi, h_lo) + dg(w_lo, h_hi)
    score = _sigmoid(logits)
    sel = score + bias_ref[...]

    neg = -jnp.inf
    s_rows = [sel[e:e + 1, :] for e in range(N_EXPERTS)]
    group_score, first, second = [], [], []
    for g in range(N_GROUPS):
        v = s_rows[g * EXP_PER_GROUP:(g + 1) * EXP_PER_GROUP]
        m1, i1 = _first_max4(v)
        v2 = [jnp.where(i1 == j, neg, v[j]) for j in range(EXP_PER_GROUP)]
        m2, i2 = _first_max4(v2)
        group_score.append(m1 + m2)
        first.append(i1)
        second.append(i2)
    _, gbest = _first_max4(group_score)
    pick = lambda rows: jnp.where(gbest == 0, rows[0], jnp.where(gbest == 1, rows[1],
                                  jnp.where(gbest == 2, rows[2], rows[3])))
    e1 = gbest * EXP_PER_GROUP + pick(first)
    e2 = gbest * EXP_PER_GROUP + pick(second)
    s1 = jnp.zeros_like(s_rows[0])
    s2 = jnp.zeros_like(s_rows[0])
    for e in range(N_EXPERTS):
        row = score[e:e + 1, :]
        s1 = s1 + jnp.where(e1 == e, row, 0.0)
        s2 = s2 + jnp.where(e2 == e, row, 0.0)
    den = s1 + s2
    return e1, e2, s1 / den, s2 / den


def _router_kernel(x_ref, mod_ref, wrt_ref, bias_ref, hf_ref, eidx_ref, wts_ref):
    m = mod_ref[0]
    h = x_ref[...] * (1.0 + m[4:5, :]) + m[3:4, :]
    hf_ref[...] = h
    e1, e2, w1, w2 = _route(h, wrt_ref, bias_ref)
    eidx_ref[0:1, :] = e1
    eidx_ref[1:2, :] = e2
    wts_ref[0:1, :] = w1
    wts_ref[1:2, :] = w2


def _router(x2, modl, wrt, bias_col):
    nt = SEQ // TM_ROUTE
    tok = lambda b, i: (b * nt + i, 0)
    tokT = lambda b, i: (0, b * nt + i)
    return pl.pallas_call(
        _router_kernel,
        grid=(BATCH, nt),
        in_specs=[
            pl.BlockSpec((TM_ROUTE, D_MODEL), tok),
            pl.BlockSpec((1, 6, D_MODEL), lambda b, i: (b, 0, 0)),
            pl.BlockSpec((N_EXPERTS, D_MODEL), lambda b, i: (0, 0)),
            pl.BlockSpec((N_EXPERTS, 1), lambda b, i: (0, 0)),
        ],
        out_specs=[
            pl.BlockSpec((TM_ROUTE, D_MODEL), tok),
            pl.BlockSpec((2, TM_ROUTE), tokT),
            pl.BlockSpec((2, TM_ROUTE), tokT),
        ],
        out_shape=[
            jax.ShapeDtypeStruct((TOKENS, D_MODEL), F32),
            jax.ShapeDtypeStruct((2, TOKENS), jnp.int32),
            jax.ShapeDtypeStruct((2, TOKENS), F32),
        ],
        compiler_params=_params("arbitrary", "arbitrary"),
        name="moe_router",
    )(x2, modl, wrt, bias_col)


def _prefix_experts(v):
    eidx = lax.broadcasted_iota(jnp.int32, v.shape, 0)
    d = 1
    while d < N_EXPERTS:
        v = v + jnp.where(eidx >= d, pltpu.roll(v, d, 0), 0.0)
        d *= 2
    return v


def _plan_kernel(e_ref, pos_ref, meta_ref, cnt_scr, run_scr, offs_scr):
    tq = TM_ROUTE
    p = pl.program_id(0)
    i = pl.program_id(1)
    e = e_ref[...]
    eid = lax.broadcasted_iota(jnp.int32, (N_EXPERTS, tq), 0)
    oh0 = jnp.where(eid == e[0:1, :], 1.0, 0.0)
    oh1 = jnp.where(eid == e[1:2, :], 1.0, 0.0)

    @pl.when(p == 0)
    def _():
        @pl.when(i == 0)
        def _():
            cnt_scr[...] = jnp.zeros_like(cnt_scr)

        cnt_scr[...] = cnt_scr[...] + jnp.sum(oh0 + oh1, axis=1, keepdims=True)

    @pl.when((p == 1) & (i == 0))
    def _():
        cnt = cnt_scr[...]
        offs = _prefix_experts(cnt) - cnt
        offs_scr[...] = offs
        run_scr[...] = jnp.zeros_like(run_scr)
        nseg = jnp.zeros_like(cnt)
        for k in range(MAX_SEG_PER_EXPERT):
            nseg = nseg + jnp.where(cnt > float(k * SEG_CAP), 1.0, 0.0)
        seg_end = _prefix_experts(nseg)
        seg_begin = seg_end - nseg
        n_seg = jnp.max(seg_end, axis=0, keepdims=True)
        s = lax.broadcasted_iota(jnp.int32, cnt.shape, 1).astype(F32)
        expert = jnp.sum(jnp.where(seg_end <= jnp.minimum(s, n_seg - 1.0), 1.0, 0.0), axis=0, keepdims=True)
        member = jnp.where(seg_begin <= s, jnp.where(s < seg_end, 1.0, 0.0), 0.0)
        done = (s - seg_begin) * float(SEG_CAP)
        rows = jnp.sum(member * jnp.minimum(float(SEG_CAP), cnt - done), axis=0, keepdims=True)
        start = jnp.sum(member * (offs + done), axis=0, keepdims=True)
        meta_ref[...] = jnp.zeros_like(meta_ref)
        meta_ref[0:1, :] = expert.astype(jnp.int32)
        meta_ref[1:2, :] = start.astype(jnp.int32)
        meta_ref[2:3, :] = rows.astype(jnp.int32)
        meta_ref[3:4, :] = n_seg.astype(jnp.int32)

    @pl.when(p == 1)
    def _():
        rj = lax.broadcasted_iota(jnp.int32, (tq, tq), 0)
        ct = lax.broadcasted_iota(jnp.int32, (tq, tq), 1)
        upper = jnp.where(rj <= ct, 1.0, 0.0).astype(BF16)
        pre0 = _dot(oh0.astype(BF16), upper)
        pre1 = _dot(oh1.astype(BF16), upper)
        tot0 = pre0[:, tq - 1:tq]
        tot1 = pre1[:, tq - 1:tq]
        base = offs_scr[:, 0:1] + run_scr[:, 0:1]
        r0 = jnp.sum(oh0 * (base + pre0 - 1.0), axis=0, keepdims=True)
        r1 = jnp.sum(oh1 * (base + tot0 + pre1 - 1.0), axis=0, keepdims=True)
        pos_ref[0:1, :] = r0.astype(jnp.int32)
        pos_ref[1:2, :] = r1.astype(jnp.int32)
        run_scr[...] = run_scr[...] + (tot0 + tot1)


def _plan(eidx):
    nt = TOKENS // TM_ROUTE
    acc = pltpu.VMEM((N_EXPERTS, LANES), F32)
    return pl.pallas_call(
        _plan_kernel,
        grid=(2, nt),
        in_specs=[pl.BlockSpec((2, TM_ROUTE), lambda p, i: (0, i))],
        out_specs=[
            pl.BlockSpec((2, TM_ROUTE), lambda p, i: (0, i * p)),
            pl.BlockSpec((8, LANES), lambda p, i: (0, 0)),
        ],
        out_shape=[
            jax.ShapeDtypeStruct((2, TOKENS), jnp.int32),
            jax.ShapeDtypeStruct((8, LANES), jnp.int32),
        ],
        scratch_shapes=[acc, acc, acc],
        compiler_params=_params("arbitrary", "arbitrary"),
        name="moe_plan",
    )(eidx)


def _invert_kernel(pos_ref, inv_ref):
    base = pl.program_id(0) * INV_CHUNK

    @pl.when(pl.program_id(0) == 0)
    def _():
        for k in range(INV_SPARE):
            inv_ref[2 * TOKENS + k] = 0

    def body(k, carry):
        inv_ref[pos_ref[k]] = base + k
        return carry

    lax.fori_loop(0, INV_CHUNK, body, 0, unroll=8)


def _invert(pos_flat):
    return pl.pallas_call(
        _invert_kernel,
        grid=(2 * TOKENS // INV_CHUNK,),
        in_specs=[pl.BlockSpec((INV_CHUNK,), lambda i: (i,), memory_space=pltpu.SMEM)],
        out_specs=pl.BlockSpec(memory_space=pltpu.SMEM),
        out_shape=jax.ShapeDtypeStruct((2 * TOKENS + INV_SPARE,), jnp.int32),
        compiler_params=_params("arbitrary"),
        name="moe_invert",
    )(pos_flat)


def _row(ref3, row):
    return ref3.at[row >> 3, pl.ds(row & 7, 1), :]


def _wait_rows(n, max_rows, src3, dst3, sem):
    p = 1
    while p <= max_rows:
        @pl.when((n & p) != 0)
        def _(p=p):
            if p < 8:
                cp = pltpu.make_async_copy(src3.at[0, pl.ds(0, p), :], dst3.at[0, pl.ds(0, p), :], sem)
            else:
                cp = pltpu.make_async_copy(src3.at[pl.ds(0, p // 8)], dst3.at[pl.ds(0, p // 8)], sem)
            cp.wait()

        p *= 2


def _ffn_kernel(se_ref, ss_ref, sr_ref, ns_ref, inv_ref, hf_hbm, wg_hbm, wu_hbm, wd_hbm, out_hbm,
                stag, wbf, xb0, xb1, act_scr, ybuf, sem_w, sem_x, sem_y, *, layer):
    s = pl.program_id(0)
    ph = pl.program_id(1)
    n_seg = ns_ref[0]
    e = se_ref[s]
    start = ss_ref[s]
    rows = sr_ref[s]
    n_sub = (rows + (SEG_SUB - 1)) >> SEG_SUB_LOG2
    half = D_MODEL // 2
    hh = D_EXPERT // 2
    nxt = jnp.minimum(s + 1, N_SEG - 1)
    next_start = ss_ref[nxt]
    next_chunks = jnp.where(s + 1 < n_seg, ((sr_ref[nxt] + (SEG_SUB - 1)) >> SEG_SUB_LOG2) * CHUNKS_PER_SUB, 0)

    def weight_copies(phase, expert, slot):
        if phase < 2:
            cols = pl.ds(phase * hh, hh)
            return [pltpu.make_async_copy(w_hbm.at[layer, expert, :, cols],
                                          stag.at[slot, :, pl.ds(c * hh, hh)], sem_w.at[slot])
                    for c, w_hbm in enumerate((wg_hbm, wu_hbm))]
        return [pltpu.make_async_copy(wd_hbm.at[layer, expert, :, pl.ds(c * half, half)],
                                      stag.at[slot, pl.ds(c * D_EXPERT, D_EXPERT), :], sem_w.at[slot])
                for c in range(2)]

    def gather_chunk(seg_start, c, xdst, sem):
        base = seg_start + c * GATHER_CHUNK
        g0 = c * (GATHER_CHUNK // 8)
        for i in range(GATHER_CHUNK // 8):
            for j in range(8):
                tok = inv_ref[base + (i * 8 + j)] & (TOKENS - 1)
                pltpu.make_async_copy(hf_hbm.at[pl.ds(tok, 1), :], xdst.at[g0 + i, pl.ds(j, 1), :], sem).start()

    blk = D_MODEL // W_SLICES

    def wait_weights(phase, slot):
        for cp in weight_copies(phase, e, slot):
            cp.wait()

    def cast_slice(kk, slot):
        w = stag[slot, kk * blk:(kk + 1) * blk, :].astype(BF16)
        wbf[kk * blk:(kk + 1) * blk, :] = w
        return w

    def gate_up_half(phase, slot, xcur, xnext, sem_next):
        first_chunk = phase * n_sub
        with_gather = jnp.clip(next_chunks - first_chunk, 0, n_sub)

        def finish(k, gu, gather):
            r0 = pl.multiple_of(k * SEG_SUB, SEG_SUB)
            g = gu[:, 0:hh]
            act_scr[pl.ds(r0, SEG_SUB), phase * hh:(phase + 1) * hh] = (
                (g * _sigmoid(g)) * gu[:, hh:2 * hh]).astype(BF16)
            if gather:
                gather_chunk(next_start, first_chunk + k, xnext, sem_next)

        def block(k, gather):
            x = xcur[pl.ds(k * (SEG_SUB // 8), SEG_SUB // 8)].reshape(SEG_SUB, D_MODEL)
            finish(k, _dot(x.astype(BF16), wbf[...]), gather)

        def first_block(gather):
            x = xcur[pl.ds(0, SEG_SUB // 8)].reshape(SEG_SUB, D_MODEL)
            gu = None
            for kk in range(W_SLICES):
                part = _dot(x[:, kk * blk:(kk + 1) * blk].astype(BF16), cast_slice(kk, slot))
                gu = part if gu is None else gu + part
            finish(0, gu, gather)

        def gathering(k, carry):
            block(k, True)
            return carry

        def plain(k, carry):
            block(k, False)
            return carry

        @pl.when(with_gather > 0)
        def _():
            first_block(True)

        @pl.when(with_gather == 0)
        def _():
            first_block(False)

        lax.fori_loop(1, jnp.maximum(with_gather, 1), gathering, 0)
        lax.fori_loop(jnp.maximum(with_gather, 1), n_sub, plain, 0)

    def leftover_chunks(seg_start, first, total, xdst, sem):
        def body(c, carry):
            gather_chunk(seg_start, c, xdst, sem)
            return carry

        lax.fori_loop(first, total, body, 0)

    def wait_gathered(xcur, sem):
        _wait_rows(n_sub * SEG_SUB, SEG_CAP, xcur, xcur, sem)

    valid = s < n_seg
    q = 3 * s + ph
    slot = q % 2

    def phase0(xcur, xnext, sem_cur, sem_next):
        @pl.when(s == 0)
        def _():
            xb0[...] = jnp.zeros_like(xb0)
            xb1[...] = jnp.zeros_like(xb1)
            for cp in weight_copies(0, e, slot):
                cp.start(priority=WEIGHT_DMA_PRIORITY)
            leftover_chunks(start, 0, n_sub * CHUNKS_PER_SUB, xcur, sem_cur)

        for cp in weight_copies(1, e, 1 - slot):
            cp.start(priority=WEIGHT_DMA_PRIORITY)
        wait_gathered(xcur, sem_cur)
        wait_weights(0, slot)
        gate_up_half(0, slot, xcur, xnext, sem_next)

    def phase1(xcur, xnext, sem_next):
        for cp in weight_copies(2, e, 1 - slot):
            cp.start(priority=WEIGHT_DMA_PRIORITY)
        wait_weights(1, slot)
        gate_up_half(1, slot, xcur, xnext, sem_next)
        leftover_chunks(next_start, 2 * n_sub, next_chunks, xnext, sem_next)

    even = (s % 2) == 0

    @pl.when(valid & (ph == 0) & even)
    def _():
        phase0(xb0, xb1, sem_x.at[0], sem_x.at[1])

    @pl.when(valid & (ph == 0) & jnp.logical_not(even))
    def _():
        phase0(xb1, xb0, sem_x.at[1], sem_x.at[0])

    @pl.when(valid & (ph == 1) & even)
    def _():
        phase1(xb0, xb1, sem_x.at[1])

    @pl.when(valid & (ph == 1) & jnp.logical_not(even))
    def _():
        phase1(xb1, xb0, sem_x.at[0])

    @pl.when(valid & (ph == 2))
    def _():
        @pl.when(s + 1 < n_seg)
        def _():
            for cp in weight_copies(0, se_ref[s + 1], 1 - slot):
                cp.start(priority=WEIGHT_DMA_PRIORITY)

        wait_weights(2, slot)

        def out_row(pair):
            return out_hbm.at[pl.ds(pair, 1), :]

        def down(k):
            r0 = pl.multiple_of(k * SEG_SUB, SEG_SUB)
            act = act_scr[pl.ds(r0, SEG_SUB), :]
            for c in range(2):
                ybuf[k % 2, :, :, c * half:(c + 1) * half] = _dot(
                    act, wbf[c * D_EXPERT:(c + 1) * D_EXPERT, :]).reshape(SEG_SUB // 8, 8, half)

        def scatter_full(k):
            base = start + k * SEG_SUB
            for i in range(SEG_SUB // 8):
                for j in range(8):
                    pltpu.make_async_copy(ybuf.at[k % 2, i, pl.ds(j, 1), :], out_row(inv_ref[base + (i * 8 + j)]),
                                          sem_y.at[k % 2]).start(priority=1)

        def wait_scattered(n, k):
            _wait_rows(n, SEG_SUB, ybuf.at[k % 2], ybuf.at[k % 2], sem_y.at[k % 2])

        def drain(seg_rows):
            subs = (seg_rows + (SEG_SUB - 1)) >> SEG_SUB_LOG2

            @pl.when(subs >= 2)
            def _():
                wait_scattered(SEG_SUB, subs)
            wait_scattered(seg_rows - (subs - 1) * SEG_SUB, subs - 1)

        @pl.when(s > 0)
        def _():
            drain(sr_ref[jnp.maximum(s - 1, 0)])

        def down_first():
            act = act_scr[0:SEG_SUB, :]
            per_half = W_SLICES // 2
            for c in range(2):
                acc = None
                for kk in range(per_half):
                    part = _dot(act[:, kk * blk:(kk + 1) * blk], cast_slice(c * per_half + kk, slot))
                    acc = part if acc is None else acc + part
                ybuf[0, :, :, c * half:(c + 1) * half] = acc.reshape(SEG_SUB // 8, 8, half)

        down_first()

        @pl.when(n_sub >= 2)
        def _():
            scatter_full(0)
            down(1)

        def steady(k, carry):
            wait_scattered(SEG_SUB, k)
            scatter_full(k - 1)
            down(k)
            return carry

        lax.fori_loop(2, n_sub, steady, 0)

        last = n_sub - 1
        n_rows = rows - last * SEG_SUB
        base = start + last * SEG_SUB

        def group(i, c):
            for j in range(8):
                pltpu.make_async_copy(ybuf.at[last % 2, i, pl.ds(j, 1), :], out_row(inv_ref[base + i * 8 + j]),
                                      sem_y.at[last % 2]).start(priority=1)
            return c

        lax.fori_loop(0, n_rows >> 3, group, 0)

        def single(r, c):
            pltpu.make_async_copy(_row(ybuf.at[last % 2], r), out_row(inv_ref[base + r]),
                                  sem_y.at[last % 2]).start(priority=1)
            return c

        lax.fori_loop((n_rows >> 3) << 3, n_rows, single, 0)

        @pl.when(s == n_seg - 1)
        def _():
            drain(rows)


def _ffn(layer, seg_expert, seg_start, seg_rows, n_seg, inv, hf, wg, wu, wd):
    assert D_EXPERT * 2 == D_MODEL
    anyspec = pl.BlockSpec(memory_space=pl.ANY)
    grid_spec = pltpu.PrefetchScalarGridSpec(
        num_scalar_prefetch=5,
        grid=(N_SEG, 3),
        in_specs=[anyspec, anyspec, anyspec, anyspec],
        out_specs=anyspec,
        scratch_shapes=[
            pltpu.VMEM((2, D_MODEL, D_EXPERT), F32),
            pltpu.VMEM((D_MODEL, D_EXPERT), BF16),
            pltpu.VMEM((SEG_CAP // 8, 8, D_MODEL), F32),
            pltpu.VMEM((SEG_CAP // 8, 8, D_MODEL), F32),
            pltpu.VMEM((SEG_CAP, D_EXPERT), BF16),
            pltpu.VMEM((2, SEG_SUB // 8, 8, D_MODEL), F32),
            pltpu.SemaphoreType.DMA((2,)),
            pltpu.SemaphoreType.DMA((2,)),
            pltpu.SemaphoreType.DMA((2,)),
        ],
    )
    return pl.pallas_call(
        functools.partial(_ffn_kernel, layer=layer),
        grid_spec=grid_spec,
        out_shape=jax.ShapeDtypeStruct((2 * TOKENS, D_MODEL), F32),
        compiler_params=_params("arbitrary", "arbitrary"),
        name="moe_ffn",
    )(seg_expert, seg_start, seg_rows, n_seg, inv, hf, wg, wu, wd)


def _combine_kernel(x_ref, mod_ref, w_ref, lng_ref, lnb_ref, y0_ref, y1_ref, o_ref):
    w = w_ref[...]
    y = w[:, 0:1] * y0_ref[...] + w[:, 1:2] * y1_ref[...]
    m = mod_ref[0]
    o_ref[...] = _ln_residual(x_ref[...], y, m[5:6, :], lng_ref[...], lnb_ref[...])


def _combine(x2, modl, wts_col, lng, lnb, y2):
    nt = SEQ // TM_COMB
    tok = lambda b, i: (b * nt + i, 0)
    tok1 = lambda b, i: (TOKENS // TM_COMB + b * nt + i, 0)
    return pl.pallas_call(
        _combine_kernel,
        grid=(BATCH, nt),
        in_specs=[
            pl.BlockSpec((TM_COMB, D_MODEL), tok),
            pl.BlockSpec((1, 6, D_MODEL), lambda b, i: (b, 0, 0)),
            pl.BlockSpec((TM_COMB, 2), tok),
            pl.BlockSpec((1, D_MODEL), lambda b, i: (0, 0)),
            pl.BlockSpec((1, D_MODEL), lambda b, i: (0, 0)),
            pl.BlockSpec((TM_COMB, D_MODEL), tok),
            pl.BlockSpec((TM_COMB, D_MODEL), tok1),
        ],
        out_specs=pl.BlockSpec((TM_COMB, D_MODEL), tok),
        out_shape=jax.ShapeDtypeStruct((TOKENS, D_MODEL), F32),
        compiler_params=_params("arbitrary", "arbitrary"),
        name="moe_combine",
    )(x2, modl, wts_col, lng, lnb, y2, y2)


def _moe(layer, x2, modl, wrt, bias_col, wg, wu, wd, lng, lnb):
    hf, eidx, wts = _router(x2, modl, wrt, bias_col)
    pos, meta = _plan(eidx)
    inv = _invert(pos.reshape(-1))
    y2 = _ffn(layer, meta[0, :N_SEG], meta[1, :N_SEG], meta[2, :N_SEG], meta[3, :1], inv, hf, wg, wu, wd)
    return _combine(x2, modl, wts.T, lng, lnb, y2)


def kernel(x, c, w_ada, b_ada, ln_mix_g, ln_mix_b, ln_ffn_g, ln_ffn_b, even_w_in, gla_w_glr, gla_b_glr, gla_norm_g, pool_w, pool_scale, even_w_out, odd_w_in, lru_conv_w, lru_conv_b, lru_w_rg, lru_b_rg, lru_w_ig, lru_b_ig, lru_lambda, odd_w_out, w_router, router_bias, exp_w_gate, exp_w_up, exp_w_down):
    mod = _adaln_mod(c, w_ada, b_ada)
    x2 = x.reshape(TOKENS, D_MODEL)
    wrt = w_router.T
    bias_col = router_bias.reshape(N_EXPERTS, 1)
    row = lambda a: a.reshape(1, -1)
    for l in range(DEPTH):
        j = l // 2
        modl = mod[l]
        if l % 2 == 0:
            w_in = even_w_in[j]
            w_main = jnp.concatenate([w_in[:, :SPLIT_R], w_in[:, SPLIT_G:]], axis=1).astype(BF16)
            wgin = jnp.pad(w_in[:, SPLIT_R:SPLIT_G], ((0, 0), (0, LANES - GLA_RANK))).astype(BF16)
            wglr = jnp.pad(gla_w_glr[j], ((0, LANES - GLA_RANK), (0, 0))).astype(BF16)
            proj = _inproj(x2, modl, w_main)
            x2 = _even_mixer(x2, proj, modl, wgin, wglr, row(gla_b_glr[j]), row(gla_norm_g[j]),
                             pool_w[j].astype(BF16), row(pool_scale[j]), even_w_out[j].astype(BF16),
                             row(ln_mix_g[l]), row(ln_mix_b[l]))
        else:
            proj = _inproj(x2, modl, odd_w_in[j].astype(BF16))
            x2 = _odd_mixer(x2, proj, modl, lru_conv_w[j], row(lru_conv_b[j]),
                            lru_w_rg[j].astype(BF16), row(lru_b_rg[j]), lru_w_ig[j].astype(BF16),
                            row(lru_b_ig[j]), row(lru_lambda[j]), odd_w_out[j].astype(BF16),
                            row(ln_mix_g[l]), row(ln_mix_b[l]))
        x2 = _moe(l, x2, modl, wrt, bias_col, exp_w_gate, exp_w_up, exp_w_down,
                  row(ln_ffn_g[l]), row(ln_ffn_b[l]))
    return x2.reshape(BATCH, SEQ, D_MODEL)
```
